```python
import math
import jax, jax.numpy as jnp
from jax import lax
import numpy as np

D_MODEL = 2048
BATCH = 8
SEQ = 2048
DEPTH = 1
DEC_BATCH = 128
DEC_SEQ = 8
PAST_LEN = 2048
PAGE_SIZE = 128

DA_HEADS = 16
DA_KV_HEADS = 8
DA_GROUP = DA_HEADS // DA_KV_HEADS
DA_DIM = 64
DA_VDIM = 2 * DA_DIM
ROPE_DIM = DA_DIM // 4
ROPE_THETA = 500000.0
HG_HEADS = 16
HG_DK = 128
HG_DV = D_MODEL // HG_HEADS
HG_CHUNK = 16
MEM_LEN = 256
CA_HEADS = 4
CA_DIM = 128
D_FF = 5632
Q_BLOCK = 128
EPS = 1e-6
W_IN_SPLITS = (DA_HEADS * 2 * DA_DIM, DA_KV_HEADS * 2 * DA_DIM, DA_KV_HEADS * DA_VDIM,
               HG_HEADS * HG_DK, HG_HEADS * HG_DK, HG_HEADS * HG_DV, HG_HEADS * HG_DV,
               D_MODEL, D_MODEL)

kernel_name = 'diffattn_hgrn2_gated_macaron_decode_step'


def rmsnorm(x, g):
    xf = x.astype(jnp.float32)
    y = xf * lax.rsqrt(jnp.mean(xf * xf, axis=-1, keepdims=True) + EPS)
    return (y * g.astype(jnp.float32)).astype(x.dtype)


def rope(x, pos):
    inv = jnp.float32(ROPE_THETA) ** (-jnp.arange(0, ROPE_DIM, 2, dtype=jnp.float32) / ROPE_DIM)
    ang = pos.astype(jnp.float32)[:, None] * inv[None, :]
    cos = jnp.cos(ang)[None, :, None, None, :]
    sin = jnp.sin(ang)[None, :, None, None, :]
    xr = x[..., :ROPE_DIM].astype(jnp.float32)
    half = ROPE_DIM // 2
    x1, x2 = xr[..., :half], xr[..., half:]
    rot = jnp.concatenate([x1 * cos - x2 * sin, x2 * cos + x1 * sin], axis=-1)
    return jnp.concatenate([rot.astype(x.dtype), x[..., ROPE_DIM:]], axis=-1)


def half_ffn(x, pre, post, w_gu, w_down):
    g, u = jnp.split(rmsnorm(x, pre) @ w_gu, 2, axis=-1)
    return x + 0.5 * rmsnorm((jax.nn.silu(g) * u) @ w_down, post)


def mixer_branches(h, pos, w_in, lb):
    n, L = h.shape[:2]
    idx = np.cumsum(np.array(W_IN_SPLITS))[:-1].tolist()
    qa, ka, va, qb, fb, ib, go, ga, gb = jnp.split(h @ w_in, idx, axis=-1)
    qa = rope(qa.reshape(n, L, DA_HEADS, 2, DA_DIM), pos)
    ka = rope(ka.reshape(n, L, DA_KV_HEADS, 2, DA_DIM), pos)
    va = va.reshape(n, L, DA_KV_HEADS, DA_VDIM)
    qb = jax.nn.silu(qb).reshape(n, L, HG_HEADS, HG_DK)
    f = lb + (1.0 - lb) * jax.nn.sigmoid(fb.astype(jnp.float32))
    logf = jnp.log(f).reshape(n, L, HG_HEADS, HG_DK)
    kb = (1.0 - f).reshape(n, L, HG_HEADS, HG_DK)
    vb = ib.reshape(n, L, HG_HEADS, HG_DV)
    return qa, ka, va, qb, logf, kb, vb, go, ga, gb


def diff_attend(q, k, v, q_pos, k_pos, lam):
    n, lq = q.shape[:2]
    qg = q.reshape(n, lq, DA_KV_HEADS, DA_GROUP, 2, DA_DIM)
    s = jnp.einsum('nqhgcd,nshcd->nhgcqs', qg, k, preferred_element_type=jnp.float32) * (DA_DIM ** -0.5)
    mask = k_pos[None, :] <= q_pos[:, None]
    p = jax.nn.softmax(jnp.where(mask, s, -jnp.inf), axis=-1)
    a = p[:, :, :, 0] - lam * p[:, :, :, 1]
    o = jnp.einsum('nhgqs,nshe->nqhge', a.astype(v.dtype), v)
    return o.reshape(n, lq, DA_HEADS, DA_VDIM)


def diff_attend_prompt(q, k, v, lam):
    n, s = q.shape[:2]
    nb = s // Q_BLOCK
    pos = jnp.arange(s)
    qb = q.reshape(n, nb, Q_BLOCK, DA_HEADS, 2, DA_DIM).swapaxes(0, 1)
    pb = pos.reshape(nb, Q_BLOCK)
    ob = lax.map(lambda a: diff_attend(a[0], k, v, a[1], pos, lam), (qb, pb))
    return ob.swapaxes(0, 1).reshape(n, s, DA_HEADS, DA_VDIM)


def hgrn2_recurrence(q, logf, k, v, s0):
    n, L = q.shape[:2]
    pad = (-L) % HG_CHUNK
    nc = (L + pad) // HG_CHUNK

    def prep(t):
        t = jnp.pad(t.astype(jnp.float32), ((0, 0), (0, pad), (0, 0), (0, 0)))
        return t.reshape(n, nc, HG_CHUNK, HG_HEADS, t.shape[-1]).transpose(1, 0, 3, 2, 4)

    qc, fc, kc, vc = prep(q), prep(logf), prep(k), prep(v)
    causal = jnp.tril(jnp.ones((HG_CHUNK, HG_CHUNK), dtype=bool))

    def step(S, inp):
        qt, lf, kt, vt = inp
        b = jnp.cumsum(lf, axis=2)
        rel = jnp.where(causal[:, :, None], b[:, :, :, None, :] - b[:, :, None, :, :], -jnp.inf)
        att = jnp.einsum('nhtd,nhtsd,nhsd->nhts', qt, jnp.exp(rel), kt)
        o = jnp.einsum('nhts,nhse->nhte', att, vt) + jnp.einsum('nhtd,nhde->nhte', qt * jnp.exp(b), S)
        b_last = b[:, :, -1:, :]
        S = jnp.exp(b_last[:, :, 0, :, None]) * S + jnp.einsum('nhsd,nhse->nhde', kt * jnp.exp(b_last - b), vt)
        return S, o

    S, oc = lax.scan(step, s0, (qc, fc, kc, vc))
    o = oc.transpose(1, 0, 3, 2, 4).reshape(n, L + pad, HG_HEADS, HG_DV)[:, :L]
    return o, S


def merge_branches(x, y_a, o_b, go, ga, gb, subln_g, hg_g, lam_init, w_out, post):
    n, L = x.shape[:2]
    y_a = (rmsnorm(y_a, subln_g) * (1.0 - lam_init)).reshape(n, L, D_MODEL)
    y_b = rmsnorm(o_b.astype(x.dtype), hg_g).reshape(n, L, D_MODEL) * jax.nn.silu(go)
    m = jax.nn.sigmoid(ga) * y_a + jax.nn.sigmoid(gb) * y_b
    return x + rmsnorm(m @ w_out, post)


def mem_kv(mem, g, w_ckv):
    n, m = mem.shape[:2]
    k, v = jnp.split(rmsnorm(mem, g) @ w_ckv, 2, axis=-1)
    return k.reshape(n, m, CA_HEADS, CA_DIM), v.reshape(n, m, CA_HEADS, CA_DIM)


def cross_attend(x, mk, mv, pre, post, w_cq, w_co):
    n, L = x.shape[:2]
    q = (rmsnorm(x, pre) @ w_cq).reshape(n, L, CA_HEADS, CA_DIM)
    s = jnp.einsum('nqhd,nshd->nhqs', q, mk, preferred_element_type=jnp.float32) * (CA_DIM ** -0.5)
    p = jax.nn.softmax(s, axis=-1)
    o = jnp.einsum('nhqs,nshd->nqhd', p.astype(mv.dtype), mv).reshape(n, L, CA_HEADS * CA_DIM)
    return x + rmsnorm(o @ w_co, post)


def setup_inputs(seed: int = 0) -> dict:
    key = jax.random.key(seed)
    keys = iter(jax.random.split(key, 64))

    def nrm(shape, scale):
        return jax.random.normal(next(keys), shape, jnp.float32) * scale

    def gain(shape):
        return 1.0 + nrm(shape, 0.02)

    n_pages = PAST_LEN // PAGE_SIZE
    n_used = DEC_BATCH * n_pages
    n_pool = n_used + n_used // 4
    page_table = jax.random.permutation(next(keys), n_pool)[:n_used].reshape(DEC_BATCH, n_pages).astype(jnp.int32)
    w_in_cols = sum(W_IN_SPLITS)
    sd = D_MODEL ** -0.5
    return {
        'x_prompt': nrm((BATCH, SEQ, D_MODEL), 1.0),
        'x_sample': nrm((DEC_BATCH, DEC_SEQ, D_MODEL), 1.0),
        'cache_k': nrm((DEPTH, n_pool, PAGE_SIZE, DA_KV_HEADS, 2 * DA_DIM), 1.0),
        'cache_v': nrm((DEPTH, n_pool, PAGE_SIZE, DA_KV_HEADS, DA_VDIM), 1.0),
        'state_hgrn': nrm((DEPTH, DEC_BATCH, HG_HEADS, HG_DK, HG_DV), 0.3),
        'cache_mem_k': nrm((DEPTH, DEC_BATCH, MEM_LEN, CA_HEADS, CA_DIM), 1.0),
        'cache_mem_v': nrm((DEPTH, DEC_BATCH, MEM_LEN, CA_HEADS, CA_DIM), 1.0),
        'page_table': page_table,
        'mem_prompt': nrm((BATCH, MEM_LEN, D_MODEL), 1.0),
        'ffn1_pre': gain((DEPTH, D_MODEL)),
        'ffn1_post': gain((DEPTH, D_MODEL)),
        'ffn1_w_gu': nrm((DEPTH, D_MODEL, 2 * D_FF), sd),
        'ffn1_w_down': nrm((DEPTH, D_FF, D_MODEL), D_FF ** -0.5),
        'mix_pre': gain((DEPTH, D_MODEL)),
        'mix_post': gain((DEPTH, D_MODEL)),
        'w_in': nrm((DEPTH, D_MODEL, w_in_cols), sd),
        'w_out': nrm((DEPTH, D_MODEL, D_MODEL), sd),
        'lambda_q1': nrm((DEPTH, DA_DIM), 0.1),
        'lambda_k1': nrm((DEPTH, DA_DIM), 0.1),
        'lambda_q2': nrm((DEPTH, DA_DIM), 0.1),
        'lambda_k2': nrm((DEPTH, DA_DIM), 0.1),
        'subln_g': gain((DEPTH, DA_VDIM)),
        'hg_norm_g': gain((DEPTH, HG_DV)),
        'hg_lb_logits': nrm((DEPTH + 1, HG_HEADS * HG_DK), 0.5),
        'ca_pre': gain((DEPTH, D_MODEL)),
        'ca_post': gain((DEPTH, D_MODEL)),
        'mem_norm_g': gain((DEPTH, D_MODEL)),
        'w_cq': nrm((DEPTH, D_MODEL, CA_HEADS * CA_DIM), sd),
        'w_ckv': nrm((DEPTH, D_MODEL, 2 * CA_HEADS * CA_DIM), sd),
        'w_co': nrm((DEPTH, CA_HEADS * CA_DIM, D_MODEL), (CA_HEADS * CA_DIM) ** -0.5),
        'ffn2_pre': gain((DEPTH, D_MODEL)),
        'ffn2_post': gain((DEPTH, D_MODEL)),
        'ffn2_w_gu': nrm((DEPTH, D_MODEL, 2 * D_FF), sd),
        'ffn2_w_down': nrm((DEPTH, D_FF, D_MODEL), D_FF ** -0.5),
    }


def reference(x_prompt, x_sample, cache_k, cache_v, state_hgrn, cache_mem_k, cache_mem_v, page_table, mem_prompt,
              ffn1_pre, ffn1_post, ffn1_w_gu, ffn1_w_down, mix_pre, mix_post, w_in, w_out,
              lambda_q1, lambda_k1, lambda_q2, lambda_k2, subln_g, hg_norm_g, hg_lb_logits,
              ca_pre, ca_post, mem_norm_g, w_cq, w_ckv, w_co,
              ffn2_pre, ffn2_post, ffn2_w_gu, ffn2_w_down):
    n_p, seq = x_prompt.shape[:2]
    n_s, dec_seq = x_sample.shape[:2]
    past = page_table.shape[1] * PAGE_SIZE
    pos_p = jnp.arange(seq)
    pos_s = past + jnp.arange(dec_seq)
    kpos_s = jnp.arange(past + dec_seq)
    lower = jnp.cumsum(jax.nn.softmax(hg_lb_logits.astype(jnp.float32), axis=0), axis=0)
    xp, xs = x_prompt, x_sample
    kp_l, vp_l, sp_l, mkp_l, mvp_l, ks_l, vs_l, ss_l = [], [], [], [], [], [], [], []
    for l in range(DEPTH):
        lam_init = 0.8 - 0.6 * math.exp(-0.3 * l)
        lam = (jnp.exp(jnp.sum(lambda_q1[l] * lambda_k1[l]).astype(jnp.float32))
               - jnp.exp(jnp.sum(lambda_q2[l] * lambda_k2[l]).astype(jnp.float32)) + lam_init)
        xp = half_ffn(xp, ffn1_pre[l], ffn1_post[l], ffn1_w_gu[l], ffn1_w_down[l])
        xs = half_ffn(xs, ffn1_pre[l], ffn1_post[l], ffn1_w_gu[l], ffn1_w_down[l])
        qa, ka, va, qb, logf, kb, vb, go, ga, gb = mixer_branches(rmsnorm(xp, mix_pre[l]), pos_p, w_in[l], lower[l])
        y_a = diff_attend_prompt(qa, ka, va, lam)
        o_b, s_p = hgrn2_recurrence(qb, logf, kb, vb, jnp.zeros((n_p, HG_HEADS, HG_DK, HG_DV), jnp.float32))
        xp = merge_branches(xp, y_a, o_b, go, ga, gb, subln_g[l], hg_norm_g[l], lam_init, w_out[l], mix_post[l])
        kp_l.append(ka.reshape(n_p, seq, DA_KV_HEADS, 2 * DA_DIM))
        vp_l.append(va)
        sp_l.append(s_p.astype(x_prompt.dtype))
        qa, ka, va, qb, logf, kb, vb, go, ga, gb = mixer_branches(rmsnorm(xs, mix_pre[l]), pos_s, w_in[l], lower[l])
        past_k = cache_k[l][page_table].reshape(n_s, past, DA_KV_HEADS, 2, DA_DIM)
        past_v = cache_v[l][page_table].reshape(n_s, past, DA_KV_HEADS, DA_VDIM)
        y_a = diff_attend(qa, jnp.concatenate([past_k, ka], axis=1), jnp.concatenate([past_v, va], axis=1),
                          pos_s, kpos_s, lam)
        o_b, s_s = hgrn2_recurrence(qb, logf, kb, vb, state_hgrn[l].astype(jnp.float32))
        xs = merge_branches(xs, y_a, o_b, go, ga, gb, subln_g[l], hg_norm_g[l], lam_init, w_out[l], mix_post[l])
        ks_l.append(ka.reshape(n_s, dec_seq, DA_KV_HEADS, 2 * DA_DIM))
        vs_l.append(va)
        ss_l.append(s_s.astype(state_hgrn.dtype))
        mk_p, mv_p = mem_kv(mem_prompt, mem_norm_g[l], w_ckv[l])
        xp = cross_attend(xp, mk_p, mv_p, ca_pre[l], ca_post[l], w_cq[l], w_co[l])
        xs = cross_attend(xs, cache_mem_k[l], cache_mem_v[l], ca_pre[l], ca_post[l], w_cq[l], w_co[l])
        mkp_l.append(mk_p)
        mvp_l.append(mv_p)
        xp = half_ffn(xp, ffn2_pre[l], ffn2_post[l], ffn2_w_gu[l], ffn2_w_down[l])
        xs = half_ffn(xs, ffn2_pre[l], ffn2_post[l], ffn2_w_gu[l], ffn2_w_down[l])
    return (xp, xs, jnp.stack(kp_l), jnp.stack(vp_l), jnp.stack(sp_l), jnp.stack(mkp_l), jnp.stack(mvp_l),
            jnp.stack(ks_l), jnp.stack(vs_l), jnp.stack(ss_l))
```

```python
import functools
import math

import jax
import jax.numpy as jnp
from jax import lax
from jax.experimental import pallas as pl
from jax.experimental.pallas import tpu as pltpu

F32 = jnp.float32
BF16 = jnp.bfloat16

EPS = 1e-6
ROPE_THETA = 500000.0
LANES = 128
BF16_ROWS = 16
VMEM_LIMIT_BYTES = 56 * 1024 * 1024

DA_HEADS = 16
DA_KV_HEADS = 8
DA_DIM = 64
ROPE_DIM = DA_DIM // 4
HG_HEADS = 16
HG_DK = 128
CA_HEADS = 4
CA_DIM = 128


def _cparams(*sem):
    return pltpu.CompilerParams(dimension_semantics=sem, vmem_limit_bytes=VMEM_LIMIT_BYTES)


def _rms(x, g):
    ms = jnp.mean(x * x, axis=-1, keepdims=True)
    return (x * lax.rsqrt(ms + EPS)) * g


def _silu(x):
    return x * jax.nn.sigmoid(x)


def _rmsnorm_kernel(x_ref, g_ref, o_ref):
    o_ref[...] = _rms(x_ref[...], g_ref[...]).astype(o_ref.dtype)


def rmsnorm_bf16(x, g, *, tm):
    rows, d = x.shape
    return pl.pallas_call(
        _rmsnorm_kernel,
        out_shape=jax.ShapeDtypeStruct((rows, d), BF16),
        grid=(rows // tm,),
        in_specs=[pl.BlockSpec((tm, d), lambda i: (i, 0)),
                  pl.BlockSpec((1, d), lambda i: (0, 0))],
        out_specs=pl.BlockSpec((tm, d), lambda i: (i, 0)),
        compiler_params=_cparams("parallel"),
        name="rmsnorm",
    )(x, g)


def _ffn_kernel(x_ref, pre_ref, post_ref, nxt_ref, wg_ref, wu_ref, wd_ref, *rest, emit_next):
    if emit_next:
        y_ref, hn_ref, h_scr, acc_scr = rest
    else:
        y_ref, h_scr, acc_scr = rest
    j = pl.program_id(1)

    @pl.when(j == 0)
    def _():
        h_scr[...] = _rms(x_ref[...], pre_ref[...]).astype(BF16)

    h = h_scr[...]
    g = jnp.dot(h, wg_ref[...], preferred_element_type=F32)
    u = jnp.dot(h, wu_ref[...], preferred_element_type=F32)
    a = (_silu(g) * u).astype(BF16)
    d = jnp.dot(a, wd_ref[...], preferred_element_type=F32)

    @pl.when(j == 0)
    def _():
        acc_scr[...] = d

    @pl.when(j > 0)
    def _():
        acc_scr[...] += d

    @pl.when(j == pl.num_programs(1) - 1)
    def _():
        y = x_ref[...] + 0.5 * _rms(acc_scr[...], post_ref[...])
        y_ref[...] = y
        if emit_next:
            hn_ref[...] = _rms(y, nxt_ref[...]).astype(BF16)


def ffn_half(x, pre, post, nxt, w_gu, w_down, *, tm, tf, emit_next):
    rows, d = x.shape
    d_ff = w_down.shape[0]
    nj = d_ff // tf
    out_shape = [jax.ShapeDtypeStruct((rows, d), F32)]
    out_specs = [pl.BlockSpec((tm, d), lambda i, j: (i, 0))]
    if emit_next:
        out_shape.append(jax.ShapeDtypeStruct((rows, d), BF16))
        out_specs.append(pl.BlockSpec((tm, d), lambda i, j: (i, 0)))
    vec = pl.BlockSpec((1, d), lambda i, j: (0, 0))
    res = pl.pallas_call(
        functools.partial(_ffn_kernel, emit_next=emit_next),
        out_shape=out_shape,
        grid=(rows // tm, nj),
        in_specs=[pl.BlockSpec((tm, d), lambda i, j: (i, 0)), vec, vec, vec,
                  pl.BlockSpec((d, tf), lambda i, j: (0, j)),
                  pl.BlockSpec((d, tf), lambda i, j: (0, j + nj)),
                  pl.BlockSpec((tf, d), lambda i, j: (j, 0))],
        out_specs=out_specs,
        scratch_shapes=[pltpu.VMEM((tm, d), BF16), pltpu.VMEM((tm, d), F32)],
        compiler_params=_cparams("parallel", "arbitrary"),
        name="ffn_half",
    )(x, pre, post, nxt, w_gu, w_gu, w_down)
    return res if emit_next else (res[0], None)


def _proj_kernel(h_ref, w_ref, *rest, rope, scale, emit_f32, emit_bf16):
    if rope:
        cos_ref, sa_ref, sb_ref = rest[:3]
        outs = rest[3:]
    else:
        outs = rest
    y = jnp.dot(h_ref[...], w_ref[...], preferred_element_type=F32)
    tn = y.shape[1]

    def emit(val, sl):
        if scale != 1.0:
            val = val * scale
        k = 0
        if emit_f32:
            outs[k][:, sl] = val
            k += 1
        if emit_bf16:
            outs[k][:, sl] = val.astype(BF16)

    if rope:
        cos, sa, sb = cos_ref[...], sa_ref[...], sb_ref[...]
        for gidx in range(tn // LANES):
            sl = slice(gidx * LANES, (gidx + 1) * LANES)
            yg = y[:, sl]
            up = pltpu.roll(yg, LANES - ROPE_DIM // 2, axis=1)
            dn = pltpu.roll(yg, ROPE_DIM // 2, axis=1)
            emit(yg * cos + up * sa + dn * sb, sl)
    else:
        emit(y, slice(None))


def project(h, w, c0, n, *, tm, tn, rope_tabs=None, rope_period=1, scale=1.0,
            emit_f32=True, emit_bf16=False):
    rows, kdim = h.shape
    cb0 = c0 // tn
    rope = rope_tabs is not None
    in_specs = [pl.BlockSpec((tm, kdim), lambda i, j: (i, 0)),
                pl.BlockSpec((kdim, tn), lambda i, j: (0, cb0 + j))]
    args = [h, w]
    if rope:
        tab = pl.BlockSpec((tm, LANES), lambda i, j: (i % rope_period, 0))
        in_specs += [tab, tab, tab]
        args += list(rope_tabs)
    out_shape, out_specs = [], []
    for flag, dt in ((emit_f32, F32), (emit_bf16, BF16)):
        if flag:
            out_shape.append(jax.ShapeDtypeStruct((rows, n), dt))
            out_specs.append(pl.BlockSpec((tm, tn), lambda i, j: (i, j)))
    return pl.pallas_call(
        functools.partial(_proj_kernel, rope=rope, scale=scale, emit_f32=emit_f32, emit_bf16=emit_bf16),
        out_shape=out_shape,
        grid=(rows // tm, n // tn),
        in_specs=in_specs,
        out_specs=out_specs,
        compiler_params=_cparams("parallel", "arbitrary"),
        name="project",
    )(*args)


def rope_tables(pos):
    half = ROPE_DIM // 2
    inv = jnp.float32(ROPE_THETA) ** (-jnp.arange(0, ROPE_DIM, 2, dtype=F32) / ROPE_DIM)
    ang = pos.astype(F32)[:, None] * inv[None, :]
    cos, sin = jnp.cos(ang), jnp.sin(ang)
    r = jnp.arange(LANES) % DA_DIM
    idx = r % half
    lo = (r < half)[None, :]
    hi = ((r >= half) & (r < ROPE_DIM))[None, :]
    cos_t = jnp.where(lo | hi, cos[:, idx], 1.0)
    sa_t = jnp.where(lo, -sin[:, idx], 0.0)
    sb_t = jnp.where(hi, sin[:, idx], 0.0)
    return cos_t, sa_t, sb_t


def _lambda(lq1, lk1, lq2, lk2, lam_init):
    s1 = jnp.sum(lq1 * lk1, axis=-1, keepdims=True)
    s2 = jnp.sum(lq2 * lk2, axis=-1, keepdims=True)
    return jnp.exp(s1) - jnp.exp(s2) + lam_init


def _stack_q(q0, q1):
    q0 = q0.astype(F32)
    q1 = q1.astype(F32)
    lane = lax.broadcasted_iota(jnp.int32, q0.shape, 1)
    first = lane < DA_DIM
    z = jnp.zeros_like(q0)
    return jnp.concatenate([jnp.where(first, q0, z), jnp.where(first, q1, z),
                            jnp.where(first, z, q0), jnp.where(first, z, q1)], axis=0).astype(BF16)


def _pad_rows(x, rows):
    if x.shape[0] == rows:
        return x
    return jnp.concatenate([x, jnp.zeros((rows - x.shape[0], x.shape[1]), x.dtype)], axis=0)


def _online_update(s, v, m_ref, l_ref, acc_ref, rows):
    m_old = m_ref[rows, :]
    m_new = jnp.maximum(m_old, jnp.max(s, axis=-1, keepdims=True))
    p = jnp.exp(s - m_new)
    alpha = jnp.exp(m_old - m_new)
    l_ref[rows, :] = alpha * l_ref[rows, :] + jnp.sum(p, axis=-1, keepdims=True)
    acc_ref[rows, :] = alpha * acc_ref[rows, :] + jnp.dot(p.astype(BF16), v, preferred_element_type=F32)
    m_ref[rows, :] = m_new


def _diff_finish(acc, l, lam, t, subln, out_scale):
    o = acc / l
    outs = []
    for g in range(2):
        d = o[g * t:(g + 1) * t] - lam * o[(2 + g) * t:(3 + g) * t]
        outs.append(_rms(d, subln) * out_scale)
    return outs


_NT = (((1,), (1,)), ((), ()))


def _attn_prompt_kernel(q_ref, k_ref, v_ref, lq1, lk1, lq2, lk2, sg_ref, o_ref,
                        qs_scr, m_scr, l_scr, acc_scr, *, tq, lam_init):
    qi = pl.program_id(2)
    qb = q_ref[0]
    qs_scr[...] = _stack_q(qb[:, :LANES], qb[:, LANES:])
    m_scr[...] = jnp.full(m_scr.shape, -jnp.inf, F32)
    l_scr[...] = jnp.zeros(l_scr.shape, F32)
    acc_scr[...] = jnp.zeros(acc_scr.shape, F32)
    allrows = slice(None)

    def body(ki, carry):
        off = pl.multiple_of(ki * tq, tq)
        k = k_ref[0, pl.ds(off, tq), :]
        v = v_ref[0, pl.ds(off, tq), :]
        s = lax.dot_general(qs_scr[...], k, _NT, preferred_element_type=F32)
        _online_update(s, v, m_scr, l_scr, acc_scr, allrows)
        return carry

    lax.fori_loop(0, qi, body, 0)

    off = pl.multiple_of(qi * tq, tq)
    k = k_ref[0, pl.ds(off, tq), :]
    v = v_ref[0, pl.ds(off, tq), :]
    s = lax.dot_general(qs_scr[...], k, _NT, preferred_element_type=F32)
    row = lax.broadcasted_iota(jnp.int32, s.shape, 0) % tq
    col = lax.broadcasted_iota(jnp.int32, s.shape, 1)
    s = jnp.where(col <= row, s, -jnp.inf)
    _online_update(s, v, m_scr, l_scr, acc_scr, allrows)

    lam = _lambda(lq1[...], lk1[...], lq2[...], lk2[...], lam_init)
    o0, o1 = _diff_finish(acc_scr[...], l_scr[...], lam, tq, sg_ref[...], 1.0 - lam_init)
    o_ref[0, :, :LANES] = o0.astype(o_ref.dtype)
    o_ref[0, :, LANES:] = o1.astype(o_ref.dtype)


def attn_prompt(q, k, v, lams, subln, lam_init, *, tq):
    n, L, _ = q.shape
    kvh = k.shape[2] // LANES
    vec = pl.BlockSpec((1, DA_DIM), lambda b, h, i: (0, 0))
    return pl.pallas_call(
        functools.partial(_attn_prompt_kernel, tq=tq, lam_init=lam_init),
        out_shape=jax.ShapeDtypeStruct(q.shape, BF16),
        grid=(n, kvh, L // tq),
        in_specs=[pl.BlockSpec((1, tq, 2 * LANES), lambda b, h, i: (b, i, h)),
                  pl.BlockSpec((1, L, LANES), lambda b, h, i: (b, 0, h)),
                  pl.BlockSpec((1, L, LANES), lambda b, h, i: (b, 0, h)),
                  vec, vec, vec, vec,
                  pl.BlockSpec((1, LANES), lambda b, h, i: (0, 0))],
        out_specs=pl.BlockSpec((1, tq, 2 * LANES), lambda b, h, i: (b, i, h)),
        scratch_shapes=[pltpu.VMEM((4 * tq, LANES), BF16), pltpu.VMEM((4 * tq, 1), F32),
                        pltpu.VMEM((4 * tq, 1), F32), pltpu.VMEM((4 * tq, LANES), F32)],
        compiler_params=_cparams("parallel", "parallel", "arbitrary"),
        name="attn_prompt",
    )(q, k, v, *lams, subln)


def _attn_sample_kernel(pt_ref, q_ref, kn_ref, vn_ref, *rest, pp, t, kvh, lam_init):
    kp = rest[:pp]
    vp = rest[pp:2 * pp]
    lq1, lk1, lq2, lk2, sg_ref, o_ref, qs_scr, m_scr, l_scr, acc_scr = rest[2 * pp:]
    step = pl.program_id(1)
    r = 4 * t

    @pl.when(step == 0)
    def _():
        for h in range(kvh):
            q0 = q_ref[0, :, (2 * h) * LANES:(2 * h + 1) * LANES]
            q1 = q_ref[0, :, (2 * h + 1) * LANES:(2 * h + 2) * LANES]
            qs_scr[h * r:(h + 1) * r, :] = _stack_q(q0, q1)
        m_scr[...] = jnp.full(m_scr.shape, -jnp.inf, F32)
        l_scr[...] = jnp.zeros(l_scr.shape, F32)
        acc_scr[...] = jnp.zeros(acc_scr.shape, F32)

    for pg in range(pp):
        for h in range(kvh):
            rows = slice(h * r, (h + 1) * r)
            k = kp[pg][0, :, h * LANES:(h + 1) * LANES].astype(BF16)
            v = vp[pg][0, :, h * LANES:(h + 1) * LANES].astype(BF16)
            s = lax.dot_general(qs_scr[rows, :], k, _NT, preferred_element_type=F32)
            _online_update(s, v, m_scr, l_scr, acc_scr, rows)

    @pl.when(step == pl.num_programs(1) - 1)
    def _():
        lam = _lambda(lq1[...], lk1[...], lq2[...], lk2[...], lam_init)
        for h in range(kvh):
            rows = slice(h * r, (h + 1) * r)
            k = _pad_rows(kn_ref[0, :, h * LANES:(h + 1) * LANES], LANES).astype(BF16)
            v = _pad_rows(vn_ref[0, :, h * LANES:(h + 1) * LANES], LANES).astype(BF16)
            s = lax.dot_general(qs_scr[rows, :], k, _NT, preferred_element_type=F32)
            row = lax.broadcasted_iota(jnp.int32, s.shape, 0) % t
            col = lax.broadcasted_iota(jnp.int32, s.shape, 1)
            s = jnp.where(col <= row, s, -jnp.inf)
            _online_update(s, v, m_scr, l_scr, acc_scr, rows)
            o0, o1 = _diff_finish(acc_scr[rows, :], l_scr[rows, :], lam, t, sg_ref[...], 1.0 - lam_init)
            o_ref[0, :, (2 * h) * LANES:(2 * h + 1) * LANES] = o0.astype(o_ref.dtype)
            o_ref[0, :, (2 * h + 1) * LANES:(2 * h + 2) * LANES] = o1.astype(o_ref.dtype)


def attn_sample(page_table, q, k_new, v_new, cache_k, cache_v, lams, subln, lam_init, *, pp):
    n, t, _ = q.shape
    kvh = k_new.shape[2] // LANES
    n_pages = page_table.shape[1]
    page = cache_k.shape[1]
    width = cache_k.shape[2]

    def page_spec(pg):
        return pl.BlockSpec((1, page, width), lambda b, s, pt: (pt[b, s * pp + pg], 0, 0))

    per_b = lambda last: pl.BlockSpec((1, t, last), lambda b, s, pt: (b, 0, 0))
    vec = pl.BlockSpec((1, DA_DIM), lambda b, s, pt: (0, 0))
    grid_spec = pltpu.PrefetchScalarGridSpec(
        num_scalar_prefetch=1,
        grid=(n, n_pages // pp),
        in_specs=[per_b(q.shape[2]), per_b(width), per_b(width)]
                 + [page_spec(pg) for pg in range(pp)] + [page_spec(pg) for pg in range(pp)]
                 + [vec, vec, vec, vec, pl.BlockSpec((1, LANES), lambda b, s, pt: (0, 0))],
        out_specs=per_b(q.shape[2]),
        scratch_shapes=[pltpu.VMEM((kvh * 4 * t, LANES), BF16), pltpu.VMEM((kvh * 4 * t, 1), F32),
                        pltpu.VMEM((kvh * 4 * t, 1), F32), pltpu.VMEM((kvh * 4 * t, LANES), F32)],
    )
    return pl.pallas_call(
        functools.partial(_attn_sample_kernel, pp=pp, t=t, kvh=kvh, lam_init=lam_init),
        out_shape=jax.ShapeDtypeStruct(q.shape, F32),
        grid_spec=grid_spec,
        compiler_params=_cparams("parallel", "arbitrary"),
        name="attn_sample",
    )(page_table, q, k_new, v_new, *([cache_k] * pp), *([cache_v] * pp), *lams, subln)


def _hgrn_chunk(qb, fb, vb, lb, s_prev, *, sub, valid):
    c_len = qb.shape[0]
    nsub = c_len // sub
    f = lb + (1.0 - lb) * jax.nn.sigmoid(fb)
    lf = jnp.log(f)
    kk = 1.0 - f
    if valid < c_len:
        live = lax.broadcasted_iota(jnp.int32, (c_len, LANES), 0) < valid
        lf = jnp.where(live, lf, 0.0)
        kk = jnp.where(live, kk, 0.0)
    qq = _silu(qb)
    r_i = lax.broadcasted_iota(jnp.int32, (c_len, c_len), 0)
    c_i = lax.broadcasted_iota(jnp.int32, (c_len, c_len), 1)
    tri = (c_i <= r_i).astype(F32)
    b = jnp.dot(tri, lf, preferred_element_type=F32, precision=lax.Precision.HIGHEST)
    vb16 = vb.astype(BF16)

    o_inter = jnp.dot((qq * jnp.exp(b)).astype(BF16), s_prev.astype(BF16), preferred_element_type=F32)

    ones = jnp.ones((LANES, LANES), BF16)
    trow = lax.broadcasted_iota(jnp.int32, (sub, LANES), 0)
    o_parts = []
    for i in range(nsub):
        lo, hi = i * sub, (i + 1) * sub
        b_i, q_i, k_i, v_i = b[lo:hi], qq[lo:hi], kk[lo:hi], vb[lo:hi]
        parts = []
        for s in range(sub):
            rel = jnp.where(trow >= s, b_i - b_i[s:s + 1, :], -jnp.inf)
            parts.append(q_i * jnp.exp(rel) * k_i[s:s + 1, :])
        p_all = jnp.concatenate(parts, axis=0).astype(BF16)
        rsum = jnp.dot(p_all, ones, preferred_element_type=F32)
        o_i = rsum[0:sub] * v_i[0:1, :]
        for s in range(1, sub):
            o_i = o_i + rsum[s * sub:(s + 1) * sub] * v_i[s:s + 1, :]
        if i > 0:
            b_ref = b[lo - 1:lo, :]
            qt = (q_i * jnp.exp(b_i - b_ref)).astype(BF16)
            kt = (kk[:lo] * jnp.exp(b_ref - b[:lo])).astype(BF16)
            att = lax.dot_general(qt, kt, _NT, preferred_element_type=F32)
            o_i = o_i + jnp.dot(att.astype(BF16), vb16[:lo], preferred_element_type=F32)
        o_parts.append(o_i)
    o = o_inter + (jnp.concatenate(o_parts, axis=0) if nsub > 1 else o_parts[0])

    b_last = b[c_len - 1:c_len, :]
    khat = (kk * jnp.exp(b_last - b)).astype(BF16)
    decay = jnp.broadcast_to(jnp.exp(b_last), (LANES, LANES)).T
    upd = lax.dot_general(khat, vb16, (((0,), (0,)), ((), ())), preferred_element_type=F32)
    return o, decay * s_prev + upd


def _hgrn_kernel(q_ref, f_ref, v_ref, lb_ref, g_ref, *rest, hb, chunk, sub, has_init):
    if has_init:
        s0_ref, o_ref, s_ref = rest
    else:
        o_ref, s_ref = rest
    tb = pl.program_id(2)

    @pl.when(tb == 0)
    def _():
        if has_init:
            s_ref[...] = s0_ref[...]
        else:
            s_ref[...] = jnp.zeros(s_ref.shape, F32)

    t_blk = q_ref.shape[1]
    if t_blk < chunk:
        for hh in range(hb):
            sl = slice(hh * LANES, (hh + 1) * LANES)
            o, s_new = _hgrn_chunk(_pad_rows(q_ref[0, :, sl], chunk), _pad_rows(f_ref[0, :, sl], chunk),
                                   _pad_rows(v_ref[0, :, sl], chunk), lb_ref[:, sl], s_ref[0, hh],
                                   sub=sub, valid=t_blk)
            s_ref[0, hh] = s_new
            o_ref[0, :, sl] = _rms(o[:t_blk], g_ref[...])
        return

    def body(ci, carry):
        off = pl.multiple_of(ci * chunk, chunk)
        for hh in range(hb):
            sl = slice(hh * LANES, (hh + 1) * LANES)
            o, s_new = _hgrn_chunk(q_ref[0, pl.ds(off, chunk), sl], f_ref[0, pl.ds(off, chunk), sl],
                                   v_ref[0, pl.ds(off, chunk), sl], lb_ref[:, sl], s_ref[0, hh],
                                   sub=sub, valid=chunk)
            s_ref[0, hh] = s_new
            o_ref[0, pl.ds(off, chunk), sl] = _rms(o, g_ref[...])
        return carry

    lax.fori_loop(0, t_blk // chunk, body, 0)


def hgrn(rest, lb, hg_g, s0, *, n, L, tb, hb, chunk, sub):
    d = HG_HEADS * HG_DK
    x3 = rest.reshape(n, L, rest.shape[1])
    nhb = HG_HEADS // hb
    w = hb * LANES
    has_init = s0 is not None

    def col_spec(seg):
        return pl.BlockSpec((1, tb, w), lambda b, h, t: (b, t, seg * (d // w) + h))

    in_specs = [col_spec(0), col_spec(1), col_spec(2),
                pl.BlockSpec((1, w), lambda b, h, t: (0, h)),
                pl.BlockSpec((1, LANES), lambda b, h, t: (0, 0))]
    args = [x3, x3, x3, lb, hg_g]
    st_spec = pl.BlockSpec((1, hb, HG_DK, LANES), lambda b, h, t: (b, h, 0, 0))
    if has_init:
        in_specs.append(st_spec)
        args.append(s0)
    return pl.pallas_call(
        functools.partial(_hgrn_kernel, hb=hb, chunk=chunk, sub=sub, has_init=has_init),
        out_shape=[jax.ShapeDtypeStruct((n, L, d), F32),
                   jax.ShapeDtypeStruct((n, HG_HEADS, HG_DK, LANES), F32)],
        grid=(n, nhb, L // tb),
        in_specs=in_specs,
        out_specs=[pl.BlockSpec((1, tb, w), lambda b, h, t: (b, t, h)), st_spec],
        compiler_params=_cparams("parallel", "parallel", "arbitrary"),
        name="hgrn",
    )(*args)


def _out_tail(a, w_ref, x_ref, post_ref, nxt_ref, y_ref, hn_ref):
    t = jnp.dot(a, w_ref[...], preferred_element_type=F32)
    y = x_ref[...] + _rms(t, post_ref[...])
    y_ref[...] = y
    hn_ref[...] = _rms(y, nxt_ref[...]).astype(BF16)


def _outproj_kernel(a_ref, w_ref, x_ref, post_ref, nxt_ref, y_ref, hn_ref):
    _out_tail(a_ref[...].astype(BF16), w_ref, x_ref, post_ref, nxt_ref, y_ref, hn_ref)


def _merge_kernel(ya_ref, yb_ref, go_ref, ga_ref, gb_ref, w_ref, x_ref, post_ref, nxt_ref, y_ref, hn_ref):
    m = (jax.nn.sigmoid(ga_ref[...]) * ya_ref[...].astype(F32)
         + jax.nn.sigmoid(gb_ref[...]) * (yb_ref[...] * _silu(go_ref[...])))
    _out_tail(m.astype(BF16), w_ref, x_ref, post_ref, nxt_ref, y_ref, hn_ref)


def _out_common(kernel, lead_args, lead_specs, w, x, post, nxt, tm, name):
    rows, d = x.shape
    kdim = w.shape[0]
    vec = pl.BlockSpec((1, d), lambda i: (0, 0))
    row = pl.BlockSpec((tm, d), lambda i: (i, 0))
    return pl.pallas_call(
        kernel,
        out_shape=[jax.ShapeDtypeStruct((rows, d), F32), jax.ShapeDtypeStruct((rows, d), BF16)],
        grid=(rows // tm,),
        in_specs=lead_specs + [pl.BlockSpec((kdim, d), lambda i: (0, 0)), row, vec, vec],
        out_specs=[row, row],
        compiler_params=_cparams("parallel"),
        name=name,
    )(*lead_args, w, x, post, nxt)


def outproj(a, w, x, post, nxt, *, tm):
    spec = pl.BlockSpec((tm, a.shape[1]), lambda i: (i, 0))
    return _out_common(_outproj_kernel, [a], [spec], w, x, post, nxt, tm, "outproj")


def merge(ya, yb, rest, w, x, post, nxt, *, tm):
    d = x.shape[1]
    row = pl.BlockSpec((tm, d), lambda i: (i, 0))
    gate = lambda seg: pl.BlockSpec((tm, d), lambda i: (i, seg))
    return _out_common(_merge_kernel, [ya, yb, rest, rest, rest], [row, row, gate(3), gate(4), gate(5)],
                       w, x, post, nxt, tm, "merge")


def _cross_kernel(q_ref, k_ref, v_ref, o_ref, *, nb, heads):
    scale = CA_DIM ** -0.5
    for b in range(nb):
        for h in range(heads):
            sl = slice(h * CA_DIM, (h + 1) * CA_DIM)
            q = q_ref[b, :, sl]
            tq = q.shape[0]
            if tq % BF16_ROWS:
                q = _pad_rows(q.astype(F32), BF16_ROWS * pl.cdiv(tq, BF16_ROWS))
            k = k_ref[b, :, sl].astype(BF16)
            v = v_ref[b, :, sl].astype(BF16)
            s = lax.dot_general(q.astype(BF16), k, _NT, preferred_element_type=F32) * scale
            p = jnp.exp(s - jnp.max(s, axis=-1, keepdims=True))
            l = jnp.sum(p, axis=-1, keepdims=True)
            o = jnp.dot(p.astype(BF16), v, preferred_element_type=F32) / l
            o_ref[b, :, sl] = o[:tq].astype(o_ref.dtype)


def cross_attend(q, mk, mv, *, nb, tq):
    n, L, w = q.shape
    m = mk.shape[1]
    qspec = pl.BlockSpec((nb, tq, w), lambda b, i: (b, i, 0))
    kspec = pl.BlockSpec((nb, m, w), lambda b, i: (b, 0, 0))
    return pl.pallas_call(
        functools.partial(_cross_kernel, nb=nb, heads=w // CA_DIM),
        out_shape=jax.ShapeDtypeStruct(q.shape, q.dtype),
        grid=(n // nb, L // tq),
        in_specs=[qspec, kspec, kspec],
        out_specs=qspec,
        compiler_params=_cparams("parallel", "arbitrary"),
        name="cross_attend",
    )(q, mk, mv)


def _row(v):
    return v.reshape(1, -1).astype(F32)


def _group_step(x, pos, n, L, lw, lower, lam_init, attn_fn, s0, mem_k, mem_v, cfg):
    d = x.shape[1]
    tm = cfg["tm"]
    x1, h1 = ffn_half(x, lw["ffn1_pre"], lw["ffn1_post"], lw["mix_pre"], lw["ffn1_w_gu"], lw["ffn1_w_down"],
                      tm=cfg["tm_ffn"], tf=cfg["tf"], emit_next=True)

    tabs = rope_tables(pos)
    period = max(L // tm, 1)
    if L < tm:
        tabs = tuple(jnp.tile(t, (tm // L, 1)) for t in tabs)
    nq = DA_HEADS * 2 * DA_DIM
    nk = DA_KV_HEADS * 2 * DA_DIM
    w_in = lw["w_in"]
    short = L % BF16_ROWS != 0
    (qa,) = project(h1, w_in, 0, nq, tm=tm, tn=cfg["tn"], rope_tabs=tabs, rope_period=period,
                    scale=DA_DIM ** -0.5, emit_f32=short, emit_bf16=not short)
    kv_out = project(h1, w_in, nq, nk, tm=tm, tn=cfg["tn"], rope_tabs=tabs, rope_period=period,
                     emit_f32=True, emit_bf16=not short)
    vv_out = project(h1, w_in, nq + nk, nk, tm=tm, tn=cfg["tn"], emit_f32=True, emit_bf16=not short)
    k32, v32 = kv_out[0], vv_out[0]
    (rest,) = project(h1, w_in, nq + 2 * nk, 6 * d, tm=tm, tn=cfg["tn"], emit_f32=True, emit_bf16=False)

    ya = attn_fn(qa.reshape(n, L, nq), kv_out[-1].reshape(n, L, nk), vv_out[-1].reshape(n, L, nk))
    yb, s_new = hgrn(rest, lower, lw["hg_norm_g"], s0, n=n, L=L, tb=cfg["hg_tb"], hb=cfg["hg_hb"],
                     chunk=cfg["hg_chunk"], sub=cfg["hg_sub"])
    x2, h2 = merge(ya.reshape(n * L, d), yb.reshape(n * L, d), rest, lw["w_out"], x1,
                   lw["mix_post"], lw["ca_pre"], tm=cfg["tm_merge"])

    (qc,) = project(h2, lw["w_cq"], 0, CA_HEADS * CA_DIM, tm=tm, tn=CA_HEADS * CA_DIM,
                    emit_f32=short, emit_bf16=not short)
    oc = cross_attend(qc.reshape(n, L, -1), mem_k, mem_v, nb=cfg["ca_nb"], tq=cfg["ca_tq"])
    x3, _ = outproj(oc.reshape(n * L, -1), lw["w_co"], x2, lw["ca_post"], lw["ffn2_pre"], tm=cfg["tm_merge"])

    x4, _ = ffn_half(x3, lw["ffn2_pre"], lw["ffn2_post"], lw["ffn2_pre"], lw["ffn2_w_gu"], lw["ffn2_w_down"],
                     tm=cfg["tm_ffn"], tf=cfg["tf"], emit_next=False)
    return x4, k32, v32, s_new


def kernel(x_prompt, x_sample, cache_k, cache_v, state_hgrn, cache_mem_k, cache_mem_v, page_table, mem_prompt, ffn1_pre, ffn1_post, ffn1_w_gu, ffn1_w_down, mix_pre, mix_post, w_in, w_out, lambda_q1, lambda_k1, lambda_q2, lambda_k2, subln_g, hg_norm_g, hg_lb_logits, ca_pre, ca_post, mem_norm_g, w_cq, w_ckv, w_co, ffn2_pre, ffn2_post, ffn2_w_gu, ffn2_w_down):
    n_p, seq, d = x_prompt.shape
    n_s, dec_seq, _ = x_sample.shape
    depth = ffn1_pre.shape[0]
    page = cache_k.shape[2]
    past = page_table.shape[1] * page
    mem_len = mem_prompt.shape[1]
    nk = DA_KV_HEADS * 2 * DA_DIM
    ca_w = CA_HEADS * CA_DIM

    lower = jnp.cumsum(jax.nn.softmax(hg_lb_logits.astype(F32), axis=0), axis=0)
    xp = x_prompt.reshape(n_p * seq, d)
    xs = x_sample.reshape(n_s * dec_seq, d)
    pos_p = jnp.arange(seq)
    pos_s = past + jnp.arange(dec_seq)

    rows_p, rows_s = n_p * seq, n_s * dec_seq
    cfg_p = dict(tm=min(1024, rows_p), tn=1024, tm_ffn=min(512, rows_p), tf=512, tm_merge=min(256, rows_p),
                 hg_tb=min(512, seq), hg_hb=2, hg_chunk=64, hg_sub=16, ca_nb=1, ca_tq=min(512, seq))
    cfg_s = dict(tm=min(1024, rows_s), tn=1024, tm_ffn=min(512, rows_s), tf=512, tm_merge=min(256, rows_s),
                 hg_tb=dec_seq, hg_hb=4, hg_chunk=BF16_ROWS, hg_sub=BF16_ROWS, ca_nb=min(8, n_s), ca_tq=dec_seq)

    outs = [[] for _ in range(8)]
    for l in range(depth):
        lam_init = 0.8 - 0.6 * math.exp(-0.3 * l)
        lw = {
            "ffn1_pre": _row(ffn1_pre[l]), "ffn1_post": _row(ffn1_post[l]),
            "ffn1_w_gu": ffn1_w_gu[l].astype(BF16), "ffn1_w_down": ffn1_w_down[l].astype(BF16),
            "mix_pre": _row(mix_pre[l]), "mix_post": _row(mix_post[l]),
            "w_in": w_in[l].astype(BF16), "w_out": w_out[l].astype(BF16),
            "hg_norm_g": _row(hg_norm_g[l]),
            "ca_pre": _row(ca_pre[l]), "ca_post": _row(ca_post[l]),
            "w_cq": w_cq[l].astype(BF16), "w_co": w_co[l].astype(BF16),
            "ffn2_pre": _row(ffn2_pre[l]), "ffn2_post": _row(ffn2_post[l]),
            "ffn2_w_gu": ffn2_w_gu[l].astype(BF16), "ffn2_w_down": ffn2_w_down[l].astype(BF16),
        }
        lams = [_row(lambda_q1[l]), _row(lambda_k1[l]), _row(lambda_q2[l]), _row(lambda_k2[l])]
        subln = _row(subln_g[l])
        lb = _row(lower[l])

        rows_m = n_p * mem_len
        hm = rmsnorm_bf16(mem_prompt.reshape(rows_m, d), _row(mem_norm_g[l]), tm=min(512, rows_m))
        w_ckv_l = w_ckv[l].astype(BF16)
        tm_m = min(1024, rows_m)
        mk32, mk16 = project(hm, w_ckv_l, 0, ca_w, tm=tm_m, tn=ca_w, emit_f32=True, emit_bf16=True)
        mv32, mv16 = project(hm, w_ckv_l, ca_w, ca_w, tm=tm_m, tn=ca_w, emit_f32=True, emit_bf16=True)

        attn_p = functools.partial(attn_prompt, lams=lams, subln=subln, lam_init=lam_init, tq=256)
        xp, kp, vp, sp = _group_step(xp, pos_p, n_p, seq, lw, lb, lam_init, attn_p, None,
                                     mk16.reshape(n_p, mem_len, ca_w), mv16.reshape(n_p, mem_len, ca_w), cfg_p)

        ck = cache_k[l].reshape(cache_k.shape[1], page, nk)
        cv = cache_v[l].reshape(cache_v.shape[1], page, nk)

        def attn_s(q, k, v):
            return attn_sample(page_table, q, k, v, ck, cv, lams, subln, lam_init, pp=4)

        xs, ks, vs, ss = _group_step(xs, pos_s, n_s, dec_seq, lw, lb, lam_init, attn_s, state_hgrn[l],
                                     cache_mem_k[l].reshape(n_s, mem_len, ca_w),
                                     cache_mem_v[l].reshape(n_s, mem_len, ca_w), cfg_s)

        for lst, val in zip(outs, (
                kp.reshape(n_p, seq, DA_KV_HEADS, 2 * DA_DIM), vp.reshape(n_p, seq, DA_KV_HEADS, 2 * DA_DIM),
                sp, mk32.reshape(n_p, mem_len, CA_HEADS, CA_DIM), mv32.reshape(n_p, mem_len, CA_HEADS, CA_DIM),
                ks.reshape(n_s, dec_seq, DA_KV_HEADS, 2 * DA_DIM), vs.reshape(n_s, dec_seq, DA_KV_HEADS, 2 * DA_DIM),
                ss)):
            lst.append(val)

    return (xp.reshape(n_p, seq, d), xs.reshape(n_s, dec_seq, d), *[jnp.stack(o) for o in outs])
```

```python
import functools
import math

import jax
import jax.numpy as jnp
from jax import lax
from jax.experimental import pallas as pl
from jax.experimental.pallas import tpu as pltpu

F32 = jnp.float32
BF16 = jnp.bfloat16

EPS = 1e-6
ROPE_THETA = 500000.0
LANES = 128
BF16_ROWS = 16
MXU_COLS = 256
VMEM_LIMIT_BYTES = 56 * 1024 * 1024
LOG2E = math.log2(math.e)

DA_HEADS = 16
DA_KV_HEADS = 8
DA_DIM = 64
ROPE_DIM = DA_DIM // 4
HG_HEADS = 16
HG_DK = 128
CA_HEADS = 4
CA_DIM = 128


def _cparams(*sem):
    return pltpu.CompilerParams(dimension_semantics=sem, vmem_limit_bytes=VMEM_LIMIT_BYTES)


def _rms(x, g):
    ms = jnp.mean(x * x, axis=-1, keepdims=True)
    return (x * lax.rsqrt(ms + EPS)) * g


def _silu(x):
    return x * jax.nn.sigmoid(x)


def _pad_rows(x, rows):
    if x.shape[0] == rows:
        return x
    return jnp.concatenate([x, jnp.zeros((rows - x.shape[0], x.shape[1]), x.dtype)], axis=0)


_NT = (((1,), (1,)), ((), ()))
_TN = (((0,), (0,)), ((), ()))


def _rmsnorm_kernel(x_ref, g_ref, o_ref):
    o_ref[...] = _rms(x_ref[...], g_ref[...]).astype(o_ref.dtype)


def rmsnorm_bf16(x, g, *, tm):
    rows, d = x.shape
    return pl.pallas_call(
        _rmsnorm_kernel,
        out_shape=jax.ShapeDtypeStruct((rows, d), BF16),
        grid=(rows // tm,),
        in_specs=[pl.BlockSpec((tm, d), lambda i: (i, 0)),
                  pl.BlockSpec((1, d), lambda i: (0, 0))],
        out_specs=pl.BlockSpec((tm, d), lambda i: (i, 0)),
        compiler_params=_cparams("parallel"),
        name="rmsnorm",
    )(x, g)


def _ffn_kernel(x_ref, pre_ref, post_ref, nxt_ref, wg_ref, wu_ref, wd_ref, *rest, emit_next):
    if emit_next:
        y_ref, hn_ref, h_scr, acc_scr = rest
    else:
        y_ref, h_scr, acc_scr = rest
    j = pl.program_id(1)

    @pl.when(j == 0)
    def _():
        h_scr[...] = _rms(x_ref[...], pre_ref[...]).astype(BF16)

    h = h_scr[...]
    g = jnp.dot(h, wg_ref[...], preferred_element_type=F32)
    u = jnp.dot(h, wu_ref[...], preferred_element_type=F32)
    a = (_silu(g) * u).astype(BF16)
    d = jnp.dot(a, wd_ref[...], preferred_element_type=F32)

    @pl.when(j == 0)
    def _():
        acc_scr[...] = d

    @pl.when(j > 0)
    def _():
        acc_scr[...] += d

    @pl.when(j == pl.num_programs(1) - 1)
    def _():
        y = x_ref[...] + 0.5 * _rms(acc_scr[...], post_ref[...])
        y_ref[...] = y
        if emit_next:
            hn_ref[...] = _rms(y, nxt_ref[...]).astype(BF16)


def ffn_half(x, pre, post, nxt, w_gu, w_down, *, tm, tf, emit_next):
    rows, d = x.shape
    d_ff = w_down.shape[0]
    nj = d_ff // tf
    out_shape = [jax.ShapeDtypeStruct((rows, d), F32)]
    out_specs = [pl.BlockSpec((tm, d), lambda i, j: (i, 0))]
    if emit_next:
        out_shape.append(jax.ShapeDtypeStruct((rows, d), BF16))
        out_specs.append(pl.BlockSpec((tm, d), lambda i, j: (i, 0)))
    vec = pl.BlockSpec((1, d), lambda i, j: (0, 0))
    res = pl.pallas_call(
        functools.partial(_ffn_kernel, emit_next=emit_next),
        out_shape=out_shape,
        grid=(rows // tm, nj),
        in_specs=[pl.BlockSpec((tm, d), lambda i, j: (i, 0)), vec, vec, vec,
                  pl.BlockSpec((d, tf), lambda i, j: (0, j)),
                  pl.BlockSpec((d, tf), lambda i, j: (0, j + nj)),
                  pl.BlockSpec((tf, d), lambda i, j: (j, 0))],
        out_specs=out_specs,
        scratch_shapes=[pltpu.VMEM((tm, d), BF16), pltpu.VMEM((tm, d), F32)],
        compiler_params=_cparams("parallel", "arbitrary"),
        name="ffn_half",
    )(x, pre, post, nxt, w_gu, w_gu, w_down)
    return res if emit_next else (res[0], None)


def _proj_kernel(h_ref, w_ref, *rest, rope, scale, emit_f32, emit_bf16):
    if rope:
        cos_ref, sa_ref, sb_ref = rest[:3]
        outs = rest[3:]
    else:
        outs = rest
    y = jnp.dot(h_ref[...], w_ref[...], preferred_element_type=F32)
    tn = y.shape[1]

    def emit(val, sl):
        if scale != 1.0:
            val = val * scale
        k = 0
        if emit_f32:
            outs[k][:, sl] = val
            k += 1
        if emit_bf16:
            outs[k][:, sl] = val.astype(BF16)

    if rope:
        cos, sa, sb = cos_ref[...], sa_ref[...], sb_ref[...]
        for gidx in range(tn // LANES):
            sl = slice(gidx * LANES, (gidx + 1) * LANES)
            yg = y[:, sl]
            up = pltpu.roll(yg, LANES - ROPE_DIM // 2, axis=1)
            dn = pltpu.roll(yg, ROPE_DIM // 2, axis=1)
            emit(yg * cos + up * sa + dn * sb, sl)
    else:
        emit(y, slice(None))


def project(h, w, c0, n, *, tm, tn, rope_tabs=None, rope_period=1, scale=1.0,
            emit_f32=True, emit_bf16=False):
    rows, kdim = h.shape
    cb0 = c0 // tn
    rope = rope_tabs is not None
    in_specs = [pl.BlockSpec((tm, kdim), lambda i, j: (i, 0)),
                pl.BlockSpec((kdim, tn), lambda i, j: (0, cb0 + j))]
    args = [h, w]
    if rope:
        tab = pl.BlockSpec((tm, LANES), lambda i, j: (i % rope_period, 0))
        in_specs += [tab, tab, tab]
        args += list(rope_tabs)
    out_shape, out_specs = [], []
    for flag, dt in ((emit_f32, F32), (emit_bf16, BF16)):
        if flag:
            out_shape.append(jax.ShapeDtypeStruct((rows, n), dt))
            out_specs.append(pl.BlockSpec((tm, tn), lambda i, j: (i, j)))
    return pl.pallas_call(
        functools.partial(_proj_kernel, rope=rope, scale=scale, emit_f32=emit_f32, emit_bf16=emit_bf16),
        out_shape=out_shape,
        grid=(rows // tm, n // tn),
        in_specs=in_specs,
        out_specs=out_specs,
        compiler_params=_cparams("parallel", "arbitrary"),
        name="project",
    )(*args)


def rope_tables(pos):
    half = ROPE_DIM // 2
    inv = jnp.float32(ROPE_THETA) ** (-jnp.arange(0, ROPE_DIM, 2, dtype=F32) / ROPE_DIM)
    ang = pos.astype(F32)[:, None] * inv[None, :]
    cos, sin = jnp.cos(ang), jnp.sin(ang)
    r = jnp.arange(LANES) % DA_DIM
    idx = r % half
    lo = (r < half)[None, :]
    hi = ((r >= half) & (r < ROPE_DIM))[None, :]
    cos_t = jnp.where(lo | hi, cos[:, idx], 1.0)
    sa_t = jnp.where(lo, -sin[:, idx], 0.0)
    sb_t = jnp.where(hi, sin[:, idx], 0.0)
    return cos_t, sa_t, sb_t


def _lambda(lq1, lk1, lq2, lk2, lam_init):
    s1 = jnp.sum(lq1 * lk1, axis=-1, keepdims=True)
    s2 = jnp.sum(lq2 * lk2, axis=-1, keepdims=True)
    return jnp.exp(s1) - jnp.exp(s2) + lam_init


def _stack_q(q0, q1):
    q0 = q0.astype(F32)
    q1 = q1.astype(F32)
    lane = lax.broadcasted_iota(jnp.int32, q0.shape, 1)
    first = lane < DA_DIM
    z = jnp.zeros_like(q0)
    return jnp.concatenate([jnp.where(first, q0, z), jnp.where(first, q1, z),
                            jnp.where(first, z, q0), jnp.where(first, z, q1)], axis=0).astype(BF16)


def _lane_tiles(x):
    return [x[:, j * LANES:(j + 1) * LANES] for j in range(x.shape[1] // LANES)]


def _diff_finish(acc, l, lam, t, subln, out_scale):
    o = acc / l
    outs = []
    for g in range(2):
        d = o[g * t:(g + 1) * t] - lam * o[(2 + g) * t:(3 + g) * t]
        outs.append(_rms(d, subln) * out_scale)
    return outs


def _attn_prompt_kernel(q_ref, k_ref, v_ref, lq1, lk1, lq2, lk2, sg_ref, o_ref,
                        qs_scr, s_scr, m_scr, l_scr, acc_scr, *, tq, lam_init):
    qi = pl.program_id(2)
    qb = q_ref[0]
    qs_scr[...] = _stack_q(qb[:, :LANES], qb[:, LANES:])

    def scores(ki):
        off = pl.multiple_of(ki * tq, tq)
        return lax.dot_general(qs_scr[...], k_ref[0, pl.ds(off, tq), :], _NT, preferred_element_type=F32)

    def lane_max(s):
        return functools.reduce(jnp.maximum, _lane_tiles(s))

    m_scr[...] = jnp.full(m_scr.shape, -jnp.inf, F32)

    def pass1(ki, carry):
        s = scores(ki)
        s_scr[ki] = s
        m_scr[...] = jnp.maximum(m_scr[...], lane_max(s))
        return carry

    lax.fori_loop(0, qi, pass1, 0)

    s = scores(qi)
    row = lax.broadcasted_iota(jnp.int32, s.shape, 0) % tq
    col = lax.broadcasted_iota(jnp.int32, s.shape, 1)
    s = jnp.where(col <= row, s, -jnp.inf)
    s_scr[qi] = s
    m_row = jnp.max(jnp.maximum(m_scr[...], lane_max(s)), axis=-1, keepdims=True)
    m_scr[...] = jnp.broadcast_to(m_row, m_scr.shape)
    l_scr[...] = jnp.zeros(l_scr.shape, F32)
    acc_scr[...] = jnp.zeros(acc_scr.shape, F32)

    def pass2(ki, carry):
        off = pl.multiple_of(ki * tq, tq)
        m_b = m_scr[...]
        ps = [jnp.exp2(st - m_b) for st in _lane_tiles(s_scr[ki])]
        l_scr[...] += functools.reduce(jnp.add, ps)
        p = jnp.concatenate(ps, axis=1).astype(BF16)
        acc_scr[...] += jnp.dot(p, v_ref[0, pl.ds(off, tq), :], preferred_element_type=F32)
        return carry

    lax.fori_loop(0, qi + 1, pass2, 0)

    lam = _lambda(lq1[...], lk1[...], lq2[...], lk2[...], lam_init)
    l = jnp.sum(l_scr[...], axis=-1, keepdims=True)
    o0, o1 = _diff_finish(acc_scr[...], l, lam, tq, sg_ref[...], 1.0 - lam_init)
    o_ref[0, :, :LANES] = o0.astype(o_ref.dtype)
    o_ref[0, :, LANES:] = o1.astype(o_ref.dtype)


def attn_prompt(q, k, v, lams, subln, lam_init, *, tq):
    n, L, _ = q.shape
    kvh = k.shape[2] // LANES
    vec = pl.BlockSpec((1, DA_DIM), lambda b, h, i: (0, 0))
    return pl.pallas_call(
        functools.partial(_attn_prompt_kernel, tq=tq, lam_init=lam_init),
        out_shape=jax.ShapeDtypeStruct(q.shape, BF16),
        grid=(n, kvh, L // tq),
        in_specs=[pl.BlockSpec((1, tq, 2 * LANES), lambda b, h, i: (b, i, h)),
                  pl.BlockSpec((1, L, LANES), lambda b, h, i: (b, 0, h)),
                  pl.BlockSpec((1, L, LANES), lambda b, h, i: (b, 0, h)),
                  vec, vec, vec, vec,
                  pl.BlockSpec((1, LANES), lambda b, h, i: (0, 0))],
        out_specs=pl.BlockSpec((1, tq, 2 * LANES), lambda b, h, i: (b, i, h)),
        scratch_shapes=[pltpu.VMEM((4 * tq, LANES), BF16), pltpu.VMEM((L // tq, 4 * tq, tq), F32),
                        pltpu.VMEM((4 * tq, LANES), F32), pltpu.VMEM((4 * tq, LANES), F32),
                        pltpu.VMEM((4 * tq, LANES), F32)],
        compiler_params=_cparams("parallel", "parallel", "arbitrary"),
        name="attn_prompt",
    )(q, k, v, *lams, subln)


def _attn_sample_kernel(pt_ref, q_ref, kn_ref, vn_ref, *rest, n_pages, t, kvh, lam_init):
    kp = rest[:n_pages]
    vp = rest[n_pages:2 * n_pages]
    lq1, lk1, lq2, lk2, sg_ref, o_ref, qs_scr, o_scr = rest[2 * n_pages:]
    r = 4 * t
    page = kp[0].shape[2] // kvh
    for h in range(kvh):
        q0 = q_ref[0, :, (2 * h) * LANES:(2 * h + 1) * LANES]
        q1 = q_ref[0, :, (2 * h + 1) * LANES:(2 * h + 2) * LANES]
        qs_scr[h * r:(h + 1) * r, :] = _stack_q(q0, q1)
    lam = _lambda(lq1[...], lk1[...], lq2[...], lk2[...], lam_init)
    row = lax.broadcasted_iota(jnp.int32, (r, LANES), 0) % t
    col = lax.broadcasted_iota(jnp.int32, (r, LANES), 1)
    new_visible = col <= row

    def head(h, carry):
        def rows_of(ref, lead, n_tok):
            return ref[lead + (pl.ds(h, n_tok, stride=kvh), slice(None))]

        qs = qs_scr[pl.ds(pl.multiple_of(h * r, r), r), :]
        kn = _pad_rows(rows_of(kn_ref, (0,), t), LANES).astype(BF16)
        vn = _pad_rows(rows_of(vn_ref, (0,), t), LANES).astype(BF16)
        tiles = [jnp.dot(qs, rows_of(kp[pg], (0, 0), page).T.astype(BF16),
                         preferred_element_type=F32) for pg in range(n_pages)]
        tiles.append(jnp.where(new_visible, lax.dot_general(qs, kn, _NT, preferred_element_type=F32),
                               -jnp.inf))
        m = jnp.max(functools.reduce(jnp.maximum, tiles), axis=-1, keepdims=True)
        ps = [jnp.exp2(st - m) for st in tiles]
        l = jnp.sum(functools.reduce(jnp.add, ps), axis=-1, keepdims=True)
        acc = jnp.dot(ps[n_pages].astype(BF16), vn, preferred_element_type=F32)
        for pg in range(n_pages):
            acc = acc + jnp.dot(ps[pg].astype(BF16), rows_of(vp[pg], (0, 0), page).astype(BF16),
                                preferred_element_type=F32)
        o0, o1 = _diff_finish(acc, l, lam, t, sg_ref[...], 1.0 - lam_init)
        o_scr[h] = jnp.concatenate([o0, o1], axis=1)
        return carry

    lax.fori_loop(0, kvh, head, 0, unroll=2)
    for h in range(kvh):
        o_ref[0, :, 2 * h * LANES:(2 * h + 2) * LANES] = o_scr[h]


def attn_sample(page_table, q, k_new, v_new, cache_k, cache_v, layer, lams, subln, lam_init):
    n, t, _ = q.shape
    kvh = k_new.shape[2]
    n_pages = page_table.shape[1]
    depth, n_pool, page = cache_k.shape[:3]
    k_new, v_new = (a.reshape(n, t * kvh, LANES) for a in (k_new, v_new))
    cache_k, cache_v = (a.reshape(depth, n_pool, page * kvh, LANES) for a in (cache_k, cache_v))

    def page_spec(pg):
        return pl.BlockSpec((1, 1, page * kvh, LANES), lambda b, pt: (layer, pt[b, pg], 0, 0))

    new_spec = pl.BlockSpec((1, t * kvh, LANES), lambda b, pt: (b, 0, 0))
    q_spec = pl.BlockSpec((1, t, q.shape[2]), lambda b, pt: (b, 0, 0))
    vec = pl.BlockSpec((1, DA_DIM), lambda b, pt: (0, 0))
    pages = [page_spec(pg) for pg in range(n_pages)]
    grid_spec = pltpu.PrefetchScalarGridSpec(
        num_scalar_prefetch=1,
        grid=(n,),
        in_specs=[q_spec, new_spec, new_spec] + pages + pages
                 + [vec, vec, vec, vec, pl.BlockSpec((1, LANES), lambda b, pt: (0, 0))],
        out_specs=q_spec,
        scratch_shapes=[pltpu.VMEM((kvh * 4 * t, LANES), BF16), pltpu.VMEM((kvh, t, 2 * LANES), F32)],
    )
    return pl.pallas_call(
        functools.partial(_attn_sample_kernel, n_pages=n_pages, t=t, kvh=kvh, lam_init=lam_init),
        out_shape=jax.ShapeDtypeStruct(q.shape, F32),
        grid_spec=grid_spec,
        compiler_params=_cparams("parallel"),
        name="attn_sample",
    )(page_table, q, k_new, v_new, *([cache_k] * n_pages), *([cache_v] * n_pages), *lams, subln)


def _split3(x):
    hi = x.astype(BF16)
    r1 = x - hi.astype(F32)
    mid = r1.astype(BF16)
    lo = (r1 - mid.astype(F32)).astype(BF16)
    return hi, mid, lo


def _hgrn_chunk(qb, fb, vb, lb, states, group_ones, *, sub, valid):
    c_len, w = qb.shape
    hb = w // LANES
    nsub = c_len // sub
    gw = group_ones.shape[0]
    f = lb + (1.0 - lb) * jax.nn.sigmoid(fb)
    if valid < c_len:
        f = jnp.where(lax.broadcasted_iota(jnp.int32, (c_len, w), 0) < valid, f, 1.0)
    lf = jnp.log(f)
    kk = 1.0 - f
    qq = _silu(qb)
    vb16 = vb.astype(BF16)

    r_i = lax.broadcasted_iota(jnp.int32, (c_len, c_len), 0)
    c_i = lax.broadcasted_iota(jnp.int32, (c_len, c_len), 1)
    tri = (c_i <= r_i).astype(BF16)
    b = functools.reduce(jnp.add, [jnp.dot(tri, part, preferred_element_type=F32) for part in _split3(lf)])
    b_last = b[c_len - 1:c_len, :]
    qe = (qq * jnp.exp(b)).astype(BF16)
    khat = (kk * jnp.exp(b_last - b)).astype(BF16)
    e_last = jnp.exp(b_last)

    trow = lax.broadcasted_iota(jnp.int32, (sub, w), 0)
    o_diag = []
    for i in range(nsub):
        lo = i * sub
        q_i, k_i, v_i, f_i = qq[lo:lo + sub], kk[lo:lo + sub], vb[lo:lo + sub], f[lo:lo + sub]
        g = jnp.where(trow == sub - 1, q_i, 0.0)
        parts = [None] * sub
        parts[sub - 1] = g * k_i[sub - 1:sub, :]
        for s in range(sub - 2, -1, -1):
            g = jnp.where(trow == s, q_i, g * f_i[s + 1:s + 2, :])
            parts[s] = g * k_i[s:s + 1, :]
        p_all = jnp.concatenate(parts, axis=0).astype(BF16)
        rsum = jnp.concatenate(
            [jnp.dot(p_all[:, j * gw:(j + 1) * gw], group_ones, preferred_element_type=F32)
             for j in range(w // gw)], axis=1)
        o_i = rsum[0:sub] * v_i[0:1, :]
        for s in range(1, sub):
            o_i = o_i + rsum[s * sub:(s + 1) * sub] * v_i[s:s + 1, :]
        o_diag.append(o_i)

    qts, kts = [None], [None]
    for i in range(1, nsub):
        lo = i * sub
        b_ref = b[lo - 1:lo, :]
        qts.append((qq[lo:lo + sub] * jnp.exp(b[lo:lo + sub] - b_ref)).astype(BF16))
        kts.append((kk[:lo] * jnp.exp(b_ref - b[:lo])).astype(BF16))

    outs, new_states = [], []
    for h in range(hb):
        sl = slice(h * LANES, (h + 1) * LANES)
        o_h = jnp.dot(qe[:, sl], states[h].astype(BF16), preferred_element_type=F32)
        rows = []
        for i in range(nsub):
            o_i = o_diag[i][:, sl]
            if i > 0:
                lo = i * sub
                att = lax.dot_general(qts[i][:, sl], kts[i][:, sl], _NT, preferred_element_type=F32)
                o_i = o_i + jnp.dot(att.astype(BF16), vb16[:lo, sl], preferred_element_type=F32)
            rows.append(o_i)
        outs.append(o_h + (jnp.concatenate(rows, axis=0) if nsub > 1 else rows[0]))
        decay = jnp.broadcast_to(e_last[:, sl], (LANES, LANES)).T
        upd = lax.dot_general(khat[:, sl], vb16[:, sl], _TN, preferred_element_type=F32)
        new_states.append(decay * states[h] + upd)
    return outs, new_states


def _hgrn_kernel(q_ref, f_ref, v_ref, lb_ref, g_ref, *rest, hb, chunk, sub, has_init):
    if has_init:
        s0_ref, o_ref, s_ref = rest
    else:
        o_ref, s_ref = rest
    tb = pl.program_id(2)

    @pl.when(tb == 0)
    def _():
        if has_init:
            s_ref[...] = s0_ref[...]
        else:
            s_ref[...] = jnp.zeros(s_ref.shape, F32)

    gw = MXU_COLS if (hb * LANES) % MXU_COLS == 0 else LANES
    gr = lax.broadcasted_iota(jnp.int32, (gw, gw), 0) // LANES
    gc = lax.broadcasted_iota(jnp.int32, (gw, gw), 1) // LANES
    group_ones = (gr == gc).astype(BF16)
    t_blk = q_ref.shape[1]

    def run(rows, valid):
        tile = lambda ref: _pad_rows(ref[0, rows, :], chunk)
        outs, new = _hgrn_chunk(tile(q_ref), tile(f_ref), tile(v_ref), lb_ref[...],
                                [s_ref[0, hh] for hh in range(hb)], group_ones, sub=sub, valid=valid)
        for hh in range(hb):
            s_ref[0, hh] = new[hh]
            o_ref[0, rows, hh * LANES:(hh + 1) * LANES] = _rms(outs[hh][:valid], g_ref[...])

    if t_blk < chunk:
        run(slice(None), t_blk)
        return

    def body(ci, carry):
        run(pl.ds(pl.multiple_of(ci * chunk, chunk), chunk), chunk)
        return carry

    lax.fori_loop(0, t_blk // chunk, body, 0)


def hgrn(rest, lb, hg_g, s0, layer, *, n, L, tb, hb, chunk, sub):
    d = HG_HEADS * HG_DK
    x3 = rest.reshape(n, L, rest.shape[1])
    nhb = HG_HEADS // hb
    w = hb * LANES
    has_init = s0 is not None

    def col_spec(seg):
        return pl.BlockSpec((1, tb, w), lambda b, h, t: (b, t, seg * (d // w) + h))

    in_specs = [col_spec(0), col_spec(1), col_spec(2),
                pl.BlockSpec((1, w), lambda b, h, t: (0, h)),
                pl.BlockSpec((1, LANES), lambda b, h, t: (0, 0))]
    args = [x3, x3, x3, lb, hg_g]
    st_spec = pl.BlockSpec((1, hb, HG_DK, LANES), lambda b, h, t: (b, h, 0, 0))
    if has_init:
        in_specs.append(pl.BlockSpec((1, 1, hb, HG_DK, LANES), lambda b, h, t: (layer, b, h, 0, 0)))
        args.append(s0)
    kernel_fn = functools.partial(_hgrn_kernel, hb=hb, chunk=chunk, sub=sub, has_init=has_init)
    if has_init:
        inner = kernel_fn

        def kernel_fn(q_ref, f_ref, v_ref, lb_ref, g_ref, s0_ref, o_ref, s_ref):
            inner(q_ref, f_ref, v_ref, lb_ref, g_ref, s0_ref.at[0], o_ref, s_ref)

    return pl.pallas_call(
        kernel_fn,
        out_shape=[jax.ShapeDtypeStruct((n, L, d), F32),
                   jax.ShapeDtypeStruct((n, HG_HEADS, HG_DK, LANES), F32)],
        grid=(n, nhb, L // tb),
        in_specs=in_specs,
        out_specs=[pl.BlockSpec((1, tb, w), lambda b, h, t: (b, t, h)), st_spec],
        compiler_params=_cparams("parallel", "parallel", "arbitrary"),
        name="hgrn",
    )(*args)


def _out_tail(a, w_ref, x_ref, post_ref, nxt_ref, y_ref, hn_ref):
    t = jnp.dot(a, w_ref[...], preferred_element_type=F32)
    y = x_ref[...] + _rms(t, post_ref[...])
    y_ref[...] = y
    hn_ref[...] = _rms(y, nxt_ref[...]).astype(BF16)


def _outproj_kernel(a_ref, w_ref, x_ref, post_ref, nxt_ref, y_ref, hn_ref):
    _out_tail(a_ref[...].astype(BF16), w_ref, x_ref, post_ref, nxt_ref, y_ref, hn_ref)


def _merge_kernel(ya_ref, yb_ref, go_ref, ga_ref, gb_ref, w_ref, x_ref, post_ref, nxt_ref, y_ref, hn_ref):
    m = (jax.nn.sigmoid(ga_ref[...]) * ya_ref[...].astype(F32)
         + jax.nn.sigmoid(gb_ref[...]) * (yb_ref[...] * _silu(go_ref[...])))
    _out_tail(m.astype(BF16), w_ref, x_ref, post_ref, nxt_ref, y_ref, hn_ref)


def _out_common(kernel, lead_args, lead_specs, w, x, post, nxt, tm, name):
    rows, d = x.shape
    kdim = w.shape[0]
    vec = pl.BlockSpec((1, d), lambda i: (0, 0))
    row = pl.BlockSpec((tm, d), lambda i: (i, 0))
    return pl.pallas_call(
        kernel,
        out_shape=[jax.ShapeDtypeStruct((rows, d), F32), jax.ShapeDtypeStruct((rows, d), BF16)],
        grid=(rows // tm,),
        in_specs=lead_specs + [pl.BlockSpec((kdim, d), lambda i: (0, 0)), row, vec, vec],
        out_specs=[row, row],
        compiler_params=_cparams("parallel"),
        name=name,
    )(*lead_args, w, x, post, nxt)


def outproj(a, w, x, post, nxt, *, tm):
    spec = pl.BlockSpec((tm, a.shape[1]), lambda i: (i, 0))
    return _out_common(_outproj_kernel, [a], [spec], w, x, post, nxt, tm, "outproj")


def merge(ya, yb, rest, w, x, post, nxt, *, tm):
    d = x.shape[1]
    row = pl.BlockSpec((tm, d), lambda i: (i, 0))
    gate = lambda seg: pl.BlockSpec((tm, d), lambda i: (i, seg))
    return _out_common(_merge_kernel, [ya, yb, rest, rest, rest], [row, row, gate(3), gate(4), gate(5)],
                       w, x, post, nxt, tm, "merge")


def _cross_kernel(q_ref, k_ref, v_ref, o_ref, *, nb, heads, native):
    scale = CA_DIM ** -0.5 * LOG2E
    for b in range(nb):
        for h in range(heads):
            sl = slice(h * CA_DIM, (h + 1) * CA_DIM)
            q = q_ref[b, :, sl]
            tq = q.shape[0]
            if tq % BF16_ROWS:
                q = _pad_rows(q.astype(F32), BF16_ROWS * pl.cdiv(tq, BF16_ROWS))
            if native:
                k, v = k_ref[0, b, :, h, :], v_ref[0, b, :, h, :]
            else:
                k, v = k_ref[b, :, sl], v_ref[b, :, sl]
            s = lax.dot_general(q.astype(BF16), k.astype(BF16), _NT, preferred_element_type=F32) * scale
            p = jnp.exp2(s - jnp.max(s, axis=-1, keepdims=True))
            l = jnp.sum(p, axis=-1, keepdims=True)
            o = jnp.dot(p.astype(BF16), v.astype(BF16), preferred_element_type=F32) / l
            o_ref[b, :, sl] = o[:tq].astype(o_ref.dtype)


def cross_attend(q, mk, mv, layer, *, nb, tq):
    n, L, w = q.shape
    native = mk.ndim == 5
    qspec = pl.BlockSpec((nb, tq, w), lambda b, i: (b, i, 0))
    if native:
        kspec = pl.BlockSpec((1, nb) + mk.shape[2:], lambda b, i: (layer, b, 0, 0, 0))
    else:
        kspec = pl.BlockSpec((nb,) + mk.shape[1:], lambda b, i: (b, 0, 0))
    return pl.pallas_call(
        functools.partial(_cross_kernel, nb=nb, heads=w // CA_DIM, native=native),
        out_shape=jax.ShapeDtypeStruct(q.shape, q.dtype),
        grid=(n // nb, L // tq),
        in_specs=[qspec, kspec, kspec],
        out_specs=qspec,
        compiler_params=_cparams("parallel", "arbitrary"),
        name="cross_attend",
    )(q, mk, mv)


def _row(v):
    return v.reshape(1, -1).astype(F32)


def _group_step(x, pos, n, L, layer, lw, lower, attn_fn, s0, mem_k, mem_v, cfg):
    d = x.shape[1]
    tm = cfg["tm"]
    x1, h1 = ffn_half(x, lw["ffn1_pre"], lw["ffn1_post"], lw["mix_pre"], lw["ffn1_w_gu"], lw["ffn1_w_down"],
                      tm=cfg["tm_ffn"], tf=cfg["tf"], emit_next=True)

    tabs = rope_tables(pos)
    period = max(L // tm, 1)
    if L < tm:
        tabs = tuple(jnp.tile(t, (tm // L, 1)) for t in tabs)
    nq = DA_HEADS * 2 * DA_DIM
    nk = DA_KV_HEADS * 2 * DA_DIM
    w_in = lw["w_in"]
    short = L % BF16_ROWS != 0
    (qa,) = project(h1, w_in, 0, nq, tm=tm, tn=cfg["tn"], rope_tabs=tabs, rope_period=period,
                    scale=DA_DIM ** -0.5 * LOG2E, emit_f32=short, emit_bf16=not short)
    kv_out = project(h1, w_in, nq, nk, tm=tm, tn=cfg["tn"], rope_tabs=tabs, rope_period=period,
                     emit_f32=True, emit_bf16=not short)
    vv_out = project(h1, w_in, nq + nk, nk, tm=tm, tn=cfg["tn"], emit_f32=True, emit_bf16=not short)
    k_rows = kv_out[0].reshape(n, L, DA_KV_HEADS, 2 * DA_DIM)
    v_rows = vv_out[0].reshape(n, L, DA_KV_HEADS, 2 * DA_DIM)
    (rest,) = project(h1, w_in, nq + 2 * nk, 6 * d, tm=tm, tn=cfg["tn"], emit_f32=True, emit_bf16=False)

    if short:
        ya = attn_fn(qa.reshape(n, L, nq), k_rows, v_rows)
    else:
        ya = attn_fn(qa.reshape(n, L, nq), kv_out[1].reshape(n, L, nk), vv_out[1].reshape(n, L, nk))
    yb, s_new = hgrn(rest, lower, lw["hg_norm_g"], s0, layer, n=n, L=L, tb=cfg["hg_tb"], hb=cfg["hg_hb"],
                     chunk=cfg["hg_chunk"], sub=cfg["hg_sub"])
    x2, h2 = merge(ya.reshape(n * L, d), yb.reshape(n * L, d), rest, lw["w_out"], x1,
                   lw["mix_post"], lw["ca_pre"], tm=cfg["tm_merge"])

    (qc,) = project(h2, lw["w_cq"], 0, CA_HEADS * CA_DIM, tm=tm, tn=CA_HEADS * CA_DIM,
                    emit_f32=short, emit_bf16=not short)
    oc = cross_attend(qc.reshape(n, L, -1), mem_k, mem_v, layer, nb=cfg["ca_nb"], tq=cfg["ca_tq"])
    x3, _ = outproj(oc.reshape(n * L, -1), lw["w_co"], x2, lw["ca_post"], lw["ffn2_pre"], tm=cfg["tm_merge"])

    x4, _ = ffn_half(x3, lw["ffn2_pre"], lw["ffn2_post"], lw["ffn2_pre"], lw["ffn2_w_gu"], lw["ffn2_w_down"],
                     tm=cfg["tm_ffn"], tf=cfg["tf"], emit_next=False)
    return x4, k_rows, v_rows, s_new


def kernel(x_prompt, x_sample, cache_k, cache_v, state_hgrn, cache_mem_k, cache_mem_v, page_table, mem_prompt, ffn1_pre, ffn1_post, ffn1_w_gu, ffn1_w_down, mix_pre, mix_post, w_in, w_out, lambda_q1, lambda_k1, lambda_q2, lambda_k2, subln_g, hg_norm_g, hg_lb_logits, ca_pre, ca_post, mem_norm_g, w_cq, w_ckv, w_co, ffn2_pre, ffn2_post, ffn2_w_gu, ffn2_w_down):
    n_p, seq, d = x_prompt.shape
    n_s, dec_seq, _ = x_sample.shape
    depth = ffn1_pre.shape[0]
    page = cache_k.shape[2]
    past = page_table.shape[1] * page
    mem_len = mem_prompt.shape[1]
    ca_w = CA_HEADS * CA_DIM

    lower = jnp.cumsum(jax.nn.softmax(hg_lb_logits.astype(F32), axis=0), axis=0)
    xp = x_prompt.reshape(n_p * seq, d)
    xs = x_sample.reshape(n_s * dec_seq, d)
    pos_p = jnp.arange(seq)
    pos_s = past + jnp.arange(dec_seq)

    rows_p, rows_s = n_p * seq, n_s * dec_seq
    cfg_p = dict(tm=min(1024, rows_p), tn=1024, tm_ffn=min(512, rows_p), tf=512, tm_merge=min(256, rows_p),
                 hg_tb=min(512, seq), hg_hb=4, hg_chunk=64, hg_sub=16, ca_nb=1, ca_tq=min(512, seq))
    cfg_s = dict(tm=min(1024, rows_s), tn=1024, tm_ffn=min(512, rows_s), tf=512, tm_merge=min(256, rows_s),
                 hg_tb=dec_seq, hg_hb=8, hg_chunk=BF16_ROWS, hg_sub=BF16_ROWS, ca_nb=min(4, n_s), ca_tq=dec_seq)

    outs = [[] for _ in range(8)]
    for l in range(depth):
        lam_init = 0.8 - 0.6 * math.exp(-0.3 * l)
        lw = {
            "ffn1_pre": _row(ffn1_pre[l]), "ffn1_post": _row(ffn1_post[l]),
            "ffn1_w_gu": ffn1_w_gu[l].astype(BF16), "ffn1_w_down": ffn1_w_down[l].astype(BF16),
            "mix_pre": _row(mix_pre[l]), "mix_post": _row(mix_post[l]),
            "w_in": w_in[l].astype(BF16), "w_out": w_out[l].astype(BF16),
            "hg_norm_g": _row(hg_norm_g[l]),
            "ca_pre": _row(ca_pre[l]), "ca_post": _row(ca_post[l]),
            "w_cq": w_cq[l].astype(BF16), "w_co": w_co[l].astype(BF16),
            "ffn2_pre": _row(ffn2_pre[l]), "ffn2_post": _row(ffn2_post[l]),
            "ffn2_w_gu": ffn2_w_gu[l].astype(BF16), "ffn2_w_down": ffn2_w_down[l].astype(BF16),
        }
        lams = [_row(lambda_q1[l]), _row(lambda_k1[l]), _row(lambda_q2[l]), _row(lambda_k2[l])]
        subln = _row(subln_g[l])
        lb = _row(lower[l])

        rows_m = n_p * mem_len
        hm = rmsnorm_bf16(mem_prompt.reshape(rows_m, d), _row(mem_norm_g[l]), tm=min(512, rows_m))
        w_ckv_l = w_ckv[l].astype(BF16)
        tm_m = min(1024, rows_m)
        mk32, mk16 = project(hm, w_ckv_l, 0, ca_w, tm=tm_m, tn=ca_w, emit_f32=True, emit_bf16=True)
        mv32, mv16 = project(hm, w_ckv_l, ca_w, ca_w, tm=tm_m, tn=ca_w, emit_f32=True, emit_bf16=True)

        attn_p = functools.partial(attn_prompt, lams=lams, subln=subln, lam_init=lam_init,
                                   tq=min(256, seq))
        xp, kp, vp, sp = _group_step(xp, pos_p, n_p, seq, l, lw, lb, attn_p, None,
                                     mk16.reshape(n_p, mem_len, ca_w), mv16.reshape(n_p, mem_len, ca_w), cfg_p)

        def attn_s(q, k, v):
            return attn_sample(page_table, q, k, v, cache_k, cache_v, l, lams, subln, lam_init)

        xs, ks, vs, ss = _group_step(xs, pos_s, n_s, dec_seq, l, lw, lb, attn_s, state_hgrn,
                                     cache_mem_k, cache_mem_v, cfg_s)

        for lst, val in zip(outs, (
                kp, vp, sp, mk32.reshape(n_p, mem_len, CA_HEADS, CA_DIM),
                mv32.reshape(n_p, mem_len, CA_HEADS, CA_DIM), ks, vs, ss)):
            lst.append(val)

    return (xp.reshape(n_p, seq, d), xs.reshape(n_s, dec_seq, d), *[jnp.stack(o) for o in outs])
```

```python
import functools
import math

import jax
import jax.numpy as jnp
from jax import lax
from jax.experimental import pallas as pl
from jax.experimental.pallas import tpu as pltpu

F32 = jnp.float32
BF16 = jnp.bfloat16

EPS = 1e-6
ROPE_THETA = 500000.0
LANES = 128
BF16_ROWS = 16
MXU_COLS = 256
VMEM_LIMIT_BYTES = 56 * 1024 * 1024
LOG2E = math.log2(math.e)

DA_HEADS = 16
DA_KV_HEADS = 8
DA_DIM = 64
ROPE_DIM = DA_DIM // 4
HG_HEADS = 16
HG_DK = 128
CA_HEADS = 4
CA_DIM = 128


def _cparams(*sem):
    return pltpu.CompilerParams(dimension_semantics=sem, vmem_limit_bytes=VMEM_LIMIT_BYTES)


def _rms(x, g):
    ms = jnp.mean(x * x, axis=-1, keepdims=True)
    return (x * lax.rsqrt(ms + EPS)) * g


def _silu(x):
    return x * jax.nn.sigmoid(x)


def _pad_rows(x, rows):
    if x.shape[0] == rows:
        return x
    return jnp.concatenate([x, jnp.zeros((rows - x.shape[0], x.shape[1]), x.dtype)], axis=0)


_NT = (((1,), (1,)), ((), ()))
_TN = (((0,), (0,)), ((), ()))


def _rmsnorm_kernel(x_ref, g_ref, o_ref):
    o_ref[...] = _rms(x_ref[...], g_ref[...]).astype(o_ref.dtype)


def rmsnorm_bf16(x, g, *, tm):
    rows, d = x.shape
    return pl.pallas_call(
        _rmsnorm_kernel,
        out_shape=jax.ShapeDtypeStruct((rows, d), BF16),
        grid=(rows // tm,),
        in_specs=[pl.BlockSpec((tm, d), lambda i: (i, 0)),
                  pl.BlockSpec((1, d), lambda i: (0, 0))],
        out_specs=pl.BlockSpec((tm, d), lambda i: (i, 0)),
        compiler_params=_cparams("parallel"),
        name="rmsnorm",
    )(x, g)


def _ffn_kernel(x_ref, pre_ref, post_ref, nxt_ref, wg_ref, wu_ref, wd_ref, *rest, emit_next):
    if emit_next:
        y_ref, hn_ref, h_scr, acc_scr = rest
    else:
        y_ref, h_scr, acc_scr = rest
    j = pl.program_id(1)

    @pl.when(j == 0)
    def _():
        h_scr[...] = _rms(x_ref[...], pre_ref[...]).astype(BF16)
        acc_scr[...] = jnp.zeros(acc_scr.shape, F32)

    h = h_scr[...]
    g = jnp.dot(h, wg_ref[...], preferred_element_type=F32)
    u = jnp.dot(h, wu_ref[...], preferred_element_type=F32)
    a = (_silu(g) * u).astype(BF16)
    cw = 2 * MXU_COLS
    for c in range(acc_scr.shape[1] // cw):
        sl = slice(c * cw, (c + 1) * cw)
        acc_scr[:, sl] += jnp.dot(a, wd_ref[:, sl], preferred_element_type=F32)

    @pl.when(j == pl.num_programs(1) - 1)
    def _():
        y = x_ref[...] + 0.5 * _rms(acc_scr[...], post_ref[...])
        y_ref[...] = y
        if emit_next:
            hn_ref[...] = _rms(y, nxt_ref[...]).astype(BF16)


def ffn_half(x, pre, post, nxt, w_gu, w_down, *, tm, tf, emit_next):
    rows, d = x.shape
    d_ff = w_down.shape[0]
    nj = d_ff // tf
    out_shape = [jax.ShapeDtypeStruct((rows, d), F32)]
    out_specs = [pl.BlockSpec((tm, d), lambda i, j: (i, 0))]
    if emit_next:
        out_shape.append(jax.ShapeDtypeStruct((rows, d), BF16))
        out_specs.append(pl.BlockSpec((tm, d), lambda i, j: (i, 0)))
    vec = pl.BlockSpec((1, d), lambda i, j: (0, 0))
    res = pl.pallas_call(
        functools.partial(_ffn_kernel, emit_next=emit_next),
        out_shape=out_shape,
        grid=(rows // tm, nj),
        in_specs=[pl.BlockSpec((tm, d), lambda i, j: (i, 0)), vec, vec, vec,
                  pl.BlockSpec((d, tf), lambda i, j: (0, j)),
                  pl.BlockSpec((d, tf), lambda i, j: (0, j + nj)),
                  pl.BlockSpec((tf, d), lambda i, j: (j, 0))],
        out_specs=out_specs,
        scratch_shapes=[pltpu.VMEM((tm, d), BF16), pltpu.VMEM((tm, d), F32)],
        compiler_params=_cparams("parallel", "arbitrary"),
        name="ffn_half",
    )(x, pre, post, nxt, w_gu, w_gu, w_down)
    return res if emit_next else (res[0], None)


def _proj_kernel(h_ref, w_ref, *rest, rope, scale, emit_f32, emit_bf16):
    if rope:
        cos_ref, sa_ref, sb_ref = rest[:3]
        outs = rest[3:]
    else:
        outs = rest
    y = jnp.dot(h_ref[...], w_ref[...], preferred_element_type=F32)
    tn = y.shape[1]

    def emit(val, sl):
        if scale != 1.0:
            val = val * scale
        k = 0
        if emit_f32:
            outs[k][:, sl] = val
            k += 1
        if emit_bf16:
            outs[k][:, sl] = val.astype(BF16)

    if rope:
        cos, sa, sb = cos_ref[...], sa_ref[...], sb_ref[...]
        for gidx in range(tn // LANES):
            sl = slice(gidx * LANES, (gidx + 1) * LANES)
            yg = y[:, sl]
            up = pltpu.roll(yg, LANES - ROPE_DIM // 2, axis=1)
            dn = pltpu.roll(yg, ROPE_DIM // 2, axis=1)
            emit(yg * cos + up * sa + dn * sb, sl)
    else:
        emit(y, slice(None))


def project(h, w, c0, n, *, tm, tn, rope_tabs=None, rope_period=1, scale=1.0,
            emit_f32=True, emit_bf16=False):
    rows, kdim = h.shape
    cb0 = c0 // tn
    rope = rope_tabs is not None
    in_specs = [pl.BlockSpec((tm, kdim), lambda i, j: (i, 0)),
                pl.BlockSpec((kdim, tn), lambda i, j: (0, cb0 + j))]
    args = [h, w]
    if rope:
        tab = pl.BlockSpec((tm, LANES), lambda i, j: (i % rope_period, 0))
        in_specs += [tab, tab, tab]
        args += list(rope_tabs)
    out_shape, out_specs = [], []
    for flag, dt in ((emit_f32, F32), (emit_bf16, BF16)):
        if flag:
            out_shape.append(jax.ShapeDtypeStruct((rows, n), dt))
            out_specs.append(pl.BlockSpec((tm, tn), lambda i, j: (i, j)))
    return pl.pallas_call(
        functools.partial(_proj_kernel, rope=rope, scale=scale, emit_f32=emit_f32, emit_bf16=emit_bf16),
        out_shape=out_shape,
        grid=(rows // tm, n // tn),
        in_specs=in_specs,
        out_specs=out_specs,
        compiler_params=_cparams("parallel", "arbitrary"),
        name="project",
    )(*args)


def rope_tables(pos):
    half = ROPE_DIM // 2
    inv = jnp.float32(ROPE_THETA) ** (-jnp.arange(0, ROPE_DIM, 2, dtype=F32) / ROPE_DIM)
    ang = pos.astype(F32)[:, None] * inv[None, :]
    cos, sin = jnp.cos(ang), jnp.sin(ang)
    r = jnp.arange(LANES) % DA_DIM
    idx = r % half
    lo = (r < half)[None, :]
    hi = ((r >= half) & (r < ROPE_DIM))[None, :]
    cos_t = jnp.where(lo | hi, cos[:, idx], 1.0)
    sa_t = jnp.where(lo, -sin[:, idx], 0.0)
    sb_t = jnp.where(hi, sin[:, idx], 0.0)
    return cos_t, sa_t, sb_t


def _lambda(lq1, lk1, lq2, lk2, lam_init):
    s1 = jnp.sum(lq1 * lk1, axis=-1, keepdims=True)
    s2 = jnp.sum(lq2 * lk2, axis=-1, keepdims=True)
    return jnp.exp(s1) - jnp.exp(s2) + lam_init


def _stack_q(q0, q1):
    q0 = q0.astype(F32)
    q1 = q1.astype(F32)
    lane = lax.broadcasted_iota(jnp.int32, q0.shape, 1)
    first = lane < DA_DIM
    z = jnp.zeros_like(q0)
    return jnp.concatenate([jnp.where(first, q0, z), jnp.where(first, q1, z),
                            jnp.where(first, z, q0), jnp.where(first, z, q1)], axis=0).astype(BF16)


def _lane_tiles(x):
    return [x[:, j * LANES:(j + 1) * LANES] for j in range(x.shape[1] // LANES)]


def _diff_finish(acc, l, lam, t, subln, out_scale):
    o = acc / l
    outs = []
    for g in range(2):
        d = o[g * t:(g + 1) * t] - lam * o[(2 + g) * t:(3 + g) * t]
        outs.append(_rms(d, subln) * out_scale)
    return outs


def _attn_prompt_kernel(q_ref, k_ref, v_ref, lq1, lk1, lq2, lk2, sg_ref, o_ref,
                        qs_scr, s_scr, m_scr, l_scr, acc_scr, *, tq, lam_init):
    qi = pl.program_id(2)
    qb = q_ref[0]
    qs_scr[...] = _stack_q(qb[:, :LANES], qb[:, LANES:])

    tk = 2 * tq
    n_full = qi // 2

    def scores(kp):
        off = pl.multiple_of(kp * tk, tk)
        return lax.dot_general(qs_scr[...], k_ref[0, pl.ds(off, tk), :], _NT, preferred_element_type=F32)

    def lane_max(s):
        return functools.reduce(jnp.maximum, _lane_tiles(s))

    m_scr[...] = jnp.full(m_scr.shape, -jnp.inf, F32)

    def pass1(kp, carry):
        s = scores(kp)
        s_scr[kp] = s
        m_scr[...] = jnp.maximum(m_scr[...], lane_max(s))
        return carry

    lax.fori_loop(0, n_full, pass1, 0)

    s = scores(n_full)
    row = lax.broadcasted_iota(jnp.int32, s.shape, 0) % tq + qi * tq
    col = lax.broadcasted_iota(jnp.int32, s.shape, 1) + n_full * tk
    s = jnp.where(col <= row, s, -jnp.inf)
    s_scr[n_full] = s
    m_row = jnp.max(jnp.maximum(m_scr[...], lane_max(s)), axis=-1, keepdims=True)
    m_scr[...] = jnp.broadcast_to(m_row, m_scr.shape)
    l_scr[...] = jnp.zeros(l_scr.shape, F32)
    acc_scr[...] = jnp.zeros(acc_scr.shape, F32)

    def accumulate(s, v):
        m_b = m_scr[...]
        ps = [jnp.exp2(st - m_b) for st in _lane_tiles(s)]
        l_scr[...] += functools.reduce(jnp.add, ps)
        p = jnp.concatenate(ps, axis=1).astype(BF16)
        acc_scr[...] += jnp.dot(p, v, preferred_element_type=F32)

    def pass2(kp, carry):
        accumulate(s_scr[kp], v_ref[0, pl.ds(pl.multiple_of(kp * tk, tk), tk), :])
        return carry

    lax.fori_loop(0, n_full, pass2, 0)
    off = pl.multiple_of(n_full * tk, tk)
    accumulate(s_scr[n_full, :, :tq], v_ref[0, pl.ds(off, tq), :])

    @pl.when(qi % 2 == 1)
    def _():
        accumulate(s_scr[n_full, :, tq:], v_ref[0, pl.ds(pl.multiple_of(off + tq, tq), tq), :])

    lam = _lambda(lq1[...], lk1[...], lq2[...], lk2[...], lam_init)
    l = jnp.sum(l_scr[...], axis=-1, keepdims=True)
    o0, o1 = _diff_finish(acc_scr[...], l, lam, tq, sg_ref[...], 1.0 - lam_init)
    o_ref[0, :, :LANES] = o0.astype(o_ref.dtype)
    o_ref[0, :, LANES:] = o1.astype(o_ref.dtype)


def attn_prompt(q, k, v, lams, subln, lam_init, *, tq):
    n, L, _ = q.shape
    kvh = k.shape[2] // LANES
    vec = pl.BlockSpec((1, DA_DIM), lambda b, h, i: (0, 0))
    return pl.pallas_call(
        functools.partial(_attn_prompt_kernel, tq=tq, lam_init=lam_init),
        out_shape=jax.ShapeDtypeStruct(q.shape, BF16),
        grid=(n, kvh, L // tq),
        in_specs=[pl.BlockSpec((1, tq, 2 * LANES), lambda b, h, i: (b, i, h)),
                  pl.BlockSpec((1, L, LANES), lambda b, h, i: (b, 0, h)),
                  pl.BlockSpec((1, L, LANES), lambda b, h, i: (b, 0, h)),
                  vec, vec, vec, vec,
                  pl.BlockSpec((1, LANES), lambda b, h, i: (0, 0))],
        out_specs=pl.BlockSpec((1, tq, 2 * LANES), lambda b, h, i: (b, i, h)),
        scratch_shapes=[pltpu.VMEM((4 * tq, LANES), BF16), pltpu.VMEM((L // (2 * tq), 4 * tq, 2 * tq), F32),
                        pltpu.VMEM((4 * tq, LANES), F32), pltpu.VMEM((4 * tq, LANES), F32),
                        pltpu.VMEM((4 * tq, LANES), F32)],
        compiler_params=_cparams("parallel", "parallel", "arbitrary"),
        name="attn_prompt",
    )(q, k, v, *lams, subln)


def _attn_sample_kernel(pt_ref, q_ref, kn_ref, vn_ref, *rest, n_pages, t, kvh, lam_init):
    kp = rest[:n_pages]
    vp = rest[n_pages:2 * n_pages]
    lq1, lk1, lq2, lk2, sg_ref, o_ref, qs_scr, o_scr = rest[2 * n_pages:]
    r = 4 * t
    page = kp[0].shape[2] // kvh
    for h in range(kvh):
        q0 = q_ref[0, :, (2 * h) * LANES:(2 * h + 1) * LANES]
        q1 = q_ref[0, :, (2 * h + 1) * LANES:(2 * h + 2) * LANES]
        qs_scr[h * r:(h + 1) * r, :] = _stack_q(q0, q1)
    lam = _lambda(lq1[...], lk1[...], lq2[...], lk2[...], lam_init)
    row = lax.broadcasted_iota(jnp.int32, (r, LANES), 0) % t
    col = lax.broadcasted_iota(jnp.int32, (r, LANES), 1)
    new_visible = col <= row

    def head(h, carry):
        def rows_of(ref, lead, n_tok):
            return ref[lead + (pl.ds(h, n_tok, stride=kvh), slice(None))]

        qs = qs_scr[pl.ds(pl.multiple_of(h * r, r), r), :]
        kn = _pad_rows(rows_of(kn_ref, (0,), t), LANES).astype(BF16)
        vn = _pad_rows(rows_of(vn_ref, (0,), t), LANES).astype(BF16)
        tiles = [jnp.dot(qs, rows_of(kp[pg], (0, 0), page).T.astype(BF16),
                         preferred_element_type=F32) for pg in range(n_pages)]
        tiles.append(jnp.where(new_visible, lax.dot_general(qs, kn, _NT, preferred_element_type=F32),
                               -jnp.inf))
        m = jnp.max(functools.reduce(jnp.maximum, tiles), axis=-1, keepdims=True)
        ps = [jnp.exp2(st - m) for st in tiles]
        l = jnp.sum(functools.reduce(jnp.add, ps), axis=-1, keepdims=True)
        acc = jnp.dot(ps[n_pages].astype(BF16), vn, preferred_element_type=F32)
        for pg in range(n_pages):
            acc = acc + jnp.dot(ps[pg].astype(BF16), rows_of(vp[pg], (0, 0), page).astype(BF16),
                                preferred_element_type=F32)
        o0, o1 = _diff_finish(acc, l, lam, t, sg_ref[...], 1.0 - lam_init)
        o_scr[h] = jnp.concatenate([o0, o1], axis=1)
        return carry

    lax.fori_loop(0, kvh, head, 0, unroll=2)
    for h in range(kvh):
        o_ref[0, :, 2 * h * LANES:(2 * h + 2) * LANES] = o_scr[h]


def attn_sample(page_table, q, k_new, v_new, cache_k, cache_v, layer, lams, subln, lam_init):
    n, t, _ = q.shape
    kvh = k_new.shape[2]
    n_pages = page_table.shape[1]
    depth, n_pool, page = cache_k.shape[:3]
    k_new, v_new = (a.reshape(n, t * kvh, LANES) for a in (k_new, v_new))
    cache_k, cache_v = (a.reshape(depth, n_pool, page * kvh, LANES) for a in (cache_k, cache_v))

    def page_spec(pg):
        return pl.BlockSpec((1, 1, page * kvh, LANES), lambda b, pt: (layer, pt[b, pg], 0, 0))

    new_spec = pl.BlockSpec((1, t * kvh, LANES), lambda b, pt: (b, 0, 0))
    q_spec = pl.BlockSpec((1, t, q.shape[2]), lambda b, pt: (b, 0, 0))
    vec = pl.BlockSpec((1, DA_DIM), lambda b, pt: (0, 0))
    pages = [page_spec(pg) for pg in range(n_pages)]
    grid_spec = pltpu.PrefetchScalarGridSpec(
        num_scalar_prefetch=1,
        grid=(n,),
        in_specs=[q_spec, new_spec, new_spec] + pages + pages
                 + [vec, vec, vec, vec, pl.BlockSpec((1, LANES), lambda b, pt: (0, 0))],
        out_specs=q_spec,
        scratch_shapes=[pltpu.VMEM((kvh * 4 * t, LANES), BF16), pltpu.VMEM((kvh, t, 2 * LANES), F32)],
    )
    return pl.pallas_call(
        functools.partial(_attn_sample_kernel, n_pages=n_pages, t=t, kvh=kvh, lam_init=lam_init),
        out_shape=jax.ShapeDtypeStruct(q.shape, F32),
        grid_spec=grid_spec,
        compiler_params=_cparams("parallel"),
        name="attn_sample",
    )(page_table, q, k_new, v_new, *([cache_k] * n_pages), *([cache_v] * n_pages), *lams, subln)


def _split3(x):
    hi = x.astype(BF16)
    r1 = x - hi.astype(F32)
    mid = r1.astype(BF16)
    lo = (r1 - mid.astype(F32)).astype(BF16)
    return hi, mid, lo


def _hgrn_chunk(qb, fb, vb, lb, states, group_ones, off_mask, f_scr, k_scr, v_scr, *, sub, valid):
    c_len, w = qb.shape
    hb = w // LANES
    nsub = c_len // sub
    gw = group_ones.shape[0]
    heads = [slice(h * LANES, (h + 1) * LANES) for h in range(hb)]
    f = lb + (1.0 - lb) * jax.nn.sigmoid(fb)
    if valid < c_len:
        f = jnp.where(lax.broadcasted_iota(jnp.int32, (c_len, w), 0) < valid, f, 1.0)
    lf = jnp.log(f)
    kk = 1.0 - f
    qq = _silu(qb)
    vb16 = vb.astype(BF16)
    for h, sl in enumerate(heads):
        f_scr[h] = f[:, sl]
        k_scr[h] = kk[:, sl]
        v_scr[h] = vb[:, sl]

    def row_bcast(ref, r):
        return jnp.concatenate([jnp.broadcast_to(ref[h, pl.ds(r, 1), :], (sub, LANES)) for h in range(hb)],
                               axis=1)

    r_i = lax.broadcasted_iota(jnp.int32, (c_len, c_len), 0)
    c_i = lax.broadcasted_iota(jnp.int32, (c_len, c_len), 1)
    tri = (c_i <= r_i).astype(BF16)
    b = functools.reduce(jnp.add, [jnp.dot(tri, part, preferred_element_type=F32) for part in _split3(lf)])
    b_last = b[c_len - 1:c_len, :]
    qe = (qq * jnp.exp(b)).astype(BF16)
    khat = (kk * jnp.exp(b_last - b)).astype(BF16)
    e_last = jnp.exp(b_last)

    if nsub > 1:
        q_parts, k_parts, v_parts = [], [], []
        for i in range(1, nsub):
            lo = i * sub
            b_ref = b[lo - 1:lo, :]
            q_parts.append((qq[lo:lo + sub] * jnp.exp(b[lo:lo + sub] - b_ref)).astype(BF16))
            k_parts.append((kk[:lo] * jnp.exp(b_ref - b[:lo])).astype(BF16))
            v_parts.append(vb16[:lo])
        q_off, k_off, v_off = (jnp.concatenate(p, axis=0) for p in (q_parts, k_parts, v_parts))
        atts = [lax.dot_general(q_off[:, sl], k_off[:, sl], _NT, preferred_element_type=F32) for sl in heads]

    o_inter = [jnp.dot(qe[:, sl], states[h].astype(BF16), preferred_element_type=F32)
               for h, sl in enumerate(heads)]
    upds = [lax.dot_general(khat[:, sl], vb16[:, sl], _TN, preferred_element_type=F32) for sl in heads]

    trow = lax.broadcasted_iota(jnp.int32, (sub, w), 0)
    o_diag = []
    for i in range(nsub):
        lo = i * sub
        q_i = qq[lo:lo + sub]
        g = jnp.where(trow == sub - 1, q_i, 0.0)
        parts = [None] * sub
        parts[sub - 1] = g * row_bcast(k_scr, lo + sub - 1)
        for s in range(sub - 2, -1, -1):
            g = jnp.where(trow == s, q_i, g * row_bcast(f_scr, lo + s + 1))
            parts[s] = g * row_bcast(k_scr, lo + s)
        p_all = jnp.concatenate(parts, axis=0).astype(BF16)
        rsum = jnp.concatenate(
            [jnp.dot(p_all[:, j * gw:(j + 1) * gw], group_ones, preferred_element_type=F32)
             for j in range(w // gw)], axis=1)
        o_i = rsum[0:sub] * row_bcast(v_scr, lo)
        for s in range(1, sub):
            o_i = o_i + rsum[s * sub:(s + 1) * sub] * row_bcast(v_scr, lo + s)
        o_diag.append(o_i)
    o_diag = jnp.concatenate(o_diag, axis=0) if nsub > 1 else o_diag[0]

    outs, new_states = [], []
    for h, sl in enumerate(heads):
        o_h = o_inter[h] + o_diag[:, sl]
        if nsub > 1:
            att = jnp.where(off_mask, atts[h], 0.0).astype(BF16)
            o_off = jnp.dot(att, v_off[:, sl], preferred_element_type=F32)
            o_h = o_h + jnp.concatenate([jnp.zeros((sub, LANES), F32), o_off], axis=0)
        outs.append(o_h)
        decay = jnp.broadcast_to(e_last[:, sl], (LANES, LANES)).T
        new_states.append(decay * states[h] + upds[h])
    return outs, new_states


def _hgrn_kernel(q_ref, f_ref, v_ref, lb_ref, g_ref, *rest, hb, chunk, sub, has_init):
    if has_init:
        s0_ref, o_ref, s_ref, f_scr, k_scr, v_scr = rest
    else:
        o_ref, s_ref, f_scr, k_scr, v_scr = rest
    tb = pl.program_id(2)

    @pl.when(tb == 0)
    def _():
        if has_init:
            s_ref[...] = s0_ref[...]
        else:
            s_ref[...] = jnp.zeros(s_ref.shape, F32)

    gw = MXU_COLS if (hb * LANES) % MXU_COLS == 0 else LANES
    gr = lax.broadcasted_iota(jnp.int32, (gw, gw), 0) // LANES
    gc = lax.broadcasted_iota(jnp.int32, (gw, gw), 1) // LANES
    group_ones = (gr == gc).astype(BF16)
    nsub = chunk // sub
    off_mask = None
    if nsub > 1:
        shape = ((nsub - 1) * sub, sub * nsub * (nsub - 1) // 2)
        qblk = lax.broadcasted_iota(jnp.int32, shape, 0) // sub + 1
        kcol = lax.broadcasted_iota(jnp.int32, shape, 1)
        off_mask = functools.reduce(jnp.logical_or, [
            (qblk == i) & (kcol >= sub * i * (i - 1) // 2) & (kcol < sub * i * (i + 1) // 2)
            for i in range(1, nsub)])
    t_blk = q_ref.shape[1]

    def run(rows, valid):
        tile = lambda ref: _pad_rows(ref[0, rows, :], chunk)
        outs, new = _hgrn_chunk(tile(q_ref), tile(f_ref), tile(v_ref), lb_ref[...],
                                [s_ref[0, hh] for hh in range(hb)], group_ones, off_mask,
                                f_scr, k_scr, v_scr, sub=sub, valid=valid)
        for hh in range(hb):
            s_ref[0, hh] = new[hh]
            o_ref[0, rows, hh * LANES:(hh + 1) * LANES] = _rms(outs[hh][:valid], g_ref[...])

    if t_blk < chunk:
        run(slice(None), t_blk)
        return

    def body(ci, carry):
        run(pl.ds(pl.multiple_of(ci * chunk, chunk), chunk), chunk)
        return carry

    lax.fori_loop(0, t_blk // chunk, body, 0)


def hgrn(rest, lb, hg_g, s0, layer, *, n, L, tb, hb, chunk, sub):
    d = HG_HEADS * HG_DK
    x3 = rest.reshape(n, L, rest.shape[1])
    nhb = HG_HEADS // hb
    w = hb * LANES
    has_init = s0 is not None

    def col_spec(seg):
        return pl.BlockSpec((1, tb, w), lambda b, h, t: (b, t, seg * (d // w) + h))

    in_specs = [col_spec(0), col_spec(1), col_spec(2),
                pl.BlockSpec((1, w), lambda b, h, t: (0, h)),
                pl.BlockSpec((1, LANES), lambda b, h, t: (0, 0))]
    args = [x3, x3, x3, lb, hg_g]
    st_spec = pl.BlockSpec((1, hb, HG_DK, LANES), lambda b, h, t: (b, h, 0, 0))
    if has_init:
        in_specs.append(pl.BlockSpec((1, 1, hb, HG_DK, LANES), lambda b, h, t: (layer, b, h, 0, 0)))
        args.append(s0)
    kernel_fn = functools.partial(_hgrn_kernel, hb=hb, chunk=chunk, sub=sub, has_init=has_init)
    if has_init:
        inner = kernel_fn

        def kernel_fn(q_ref, f_ref, v_ref, lb_ref, g_ref, s0_ref, *rest):
            inner(q_ref, f_ref, v_ref, lb_ref, g_ref, s0_ref.at[0], *rest)

    return pl.pallas_call(
        kernel_fn,
        out_shape=[jax.ShapeDtypeStruct((n, L, d), F32),
                   jax.ShapeDtypeStruct((n, HG_HEADS, HG_DK, LANES), F32)],
        grid=(n, nhb, L // tb),
        in_specs=in_specs,
        out_specs=[pl.BlockSpec((1, tb, w), lambda b, h, t: (b, t, h)), st_spec],
        scratch_shapes=[pltpu.VMEM((hb, chunk, LANES), F32)] * 3,
        compiler_params=_cparams("parallel", "parallel", "arbitrary"),
        name="hgrn",
    )(*args)


def _out_tail(a, w_ref, x_ref, post_ref, nxt_ref, y_ref, hn_ref):
    t = jnp.dot(a, w_ref[...], preferred_element_type=F32)
    y = x_ref[...] + _rms(t, post_ref[...])
    y_ref[...] = y
    hn_ref[...] = _rms(y, nxt_ref[...]).astype(BF16)


def _outproj_kernel(a_ref, w_ref, x_ref, post_ref, nxt_ref, y_ref, hn_ref):
    _out_tail(a_ref[...].astype(BF16), w_ref, x_ref, post_ref, nxt_ref, y_ref, hn_ref)


def _merge_kernel(ya_ref, yb_ref, go_ref, ga_ref, gb_ref, w_ref, x_ref, post_ref, nxt_ref, y_ref, hn_ref):
    m = (jax.nn.sigmoid(ga_ref[...]) * ya_ref[...].astype(F32)
         + jax.nn.sigmoid(gb_ref[...]) * (yb_ref[...] * _silu(go_ref[...])))
    _out_tail(m.astype(BF16), w_ref, x_ref, post_ref, nxt_ref, y_ref, hn_ref)


def _out_common(kernel, lead_args, lead_specs, w, x, post, nxt, tm, name):
    rows, d = x.shape
    kdim = w.shape[0]
    vec = pl.BlockSpec((1, d), lambda i: (0, 0))
    row = pl.BlockSpec((tm, d), lambda i: (i, 0))
    return pl.pallas_call(
        kernel,
        out_shape=[jax.ShapeDtypeStruct((rows, d), F32), jax.ShapeDtypeStruct((rows, d), BF16)],
        grid=(rows // tm,),
        in_specs=lead_specs + [pl.BlockSpec((kdim, d), lambda i: (0, 0)), row, vec, vec],
        out_specs=[row, row],
        compiler_params=_cparams("parallel"),
        name=name,
    )(*lead_args, w, x, post, nxt)


def outproj(a, w, x, post, nxt, *, tm):
    spec = pl.BlockSpec((tm, a.shape[1]), lambda i: (i, 0))
    return _out_common(_outproj_kernel, [a], [spec], w, x, post, nxt, tm, "outproj")


def merge(ya, yb, rest, w, x, post, nxt, *, tm):
    d = x.shape[1]
    row = pl.BlockSpec((tm, d), lambda i: (i, 0))
    gate = lambda seg: pl.BlockSpec((tm, d), lambda i: (i, seg))
    return _out_common(_merge_kernel, [ya, yb, rest, rest, rest], [row, row, gate(3), gate(4), gate(5)],
                       w, x, post, nxt, tm, "merge")


def _cross_kernel(q_ref, k_ref, v_ref, o_ref, *, nb, heads, native):
    scale = CA_DIM ** -0.5 * LOG2E
    for b in range(nb):
        for h in range(heads):
            sl = slice(h * CA_DIM, (h + 1) * CA_DIM)
            q = q_ref[b, :, sl]
            tq = q.shape[0]
            if tq % BF16_ROWS:
                q = _pad_rows(q.astype(F32), BF16_ROWS * pl.cdiv(tq, BF16_ROWS))
            if native:
                k, v = k_ref[0, b, :, h, :], v_ref[0, b, :, h, :]
            else:
                k, v = k_ref[b, :, sl], v_ref[b, :, sl]
            s = lax.dot_general(q.astype(BF16), k.astype(BF16), _NT, preferred_element_type=F32) * scale
            p = jnp.exp2(s - jnp.max(s, axis=-1, keepdims=True))
            l = jnp.sum(p, axis=-1, keepdims=True)
            o = jnp.dot(p.astype(BF16), v.astype(BF16), preferred_element_type=F32) / l
            o_ref[b, :, sl] = o[:tq].astype(o_ref.dtype)


def cross_attend(q, mk, mv, layer, *, nb, tq):
    n, L, w = q.shape
    native = mk.ndim == 5
    qspec = pl.BlockSpec((nb, tq, w), lambda b, i: (b, i, 0))
    if native:
        kspec = pl.BlockSpec((1, nb) + mk.shape[2:], lambda b, i: (layer, b, 0, 0, 0))
    else:
        kspec = pl.BlockSpec((nb,) + mk.shape[1:], lambda b, i: (b, 0, 0))
    return pl.pallas_call(
        functools.partial(_cross_kernel, nb=nb, heads=w // CA_DIM, native=native),
        out_shape=jax.ShapeDtypeStruct(q.shape, q.dtype),
        grid=(n // nb, L // tq),
        in_specs=[qspec, kspec, kspec],
        out_specs=qspec,
        compiler_params=_cparams("parallel", "arbitrary"),
        name="cross_attend",
    )(q, mk, mv)


def _row(v):
    return v.reshape(1, -1).astype(F32)


def _group_step(x, pos, n, L, layer, lw, lower, attn_fn, s0, mem_k, mem_v, cfg):
    d = x.shape[1]
    tm = cfg["tm"]
    x1, h1 = ffn_half(x, lw["ffn1_pre"], lw["ffn1_post"], lw["mix_pre"], lw["ffn1_w_gu"], lw["ffn1_w_down"],
                      tm=cfg["tm_ffn"], tf=cfg["tf"], emit_next=True)

    tabs = rope_tables(pos)
    period = max(L // tm, 1)
    if L < tm:
        tabs = tuple(jnp.tile(t, (tm // L, 1)) for t in tabs)
    nq = DA_HEADS * 2 * DA_DIM
    nk = DA_KV_HEADS * 2 * DA_DIM
    w_in = lw["w_in"]
    short = L % BF16_ROWS != 0
    (qa,) = project(h1, w_in, 0, nq, tm=tm, tn=cfg["tn"], rope_tabs=tabs, rope_period=period,
                    scale=DA_DIM ** -0.5 * LOG2E, emit_f32=short, emit_bf16=not short)
    kv_out = project(h1, w_in, nq, nk, tm=tm, tn=cfg["tn"], rope_tabs=tabs, rope_period=period,
                     emit_f32=True, emit_bf16=not short)
    vv_out = project(h1, w_in, nq + nk, nk, tm=tm, tn=cfg["tn"], emit_f32=True, emit_bf16=not short)
    k_rows = kv_out[0].reshape(n, L, DA_KV_HEADS, 2 * DA_DIM)
    v_rows = vv_out[0].reshape(n, L, DA_KV_HEADS, 2 * DA_DIM)
    (rest,) = project(h1, w_in, nq + 2 * nk, 6 * d, tm=tm, tn=cfg["tn"], emit_f32=True, emit_bf16=False)

    if short:
        ya = attn_fn(qa.reshape(n, L, nq), k_rows, v_rows)
    else:
        ya = attn_fn(qa.reshape(n, L, nq), kv_out[1].reshape(n, L, nk), vv_out[1].reshape(n, L, nk))
    yb, s_new = hgrn(rest, lower, lw["hg_norm_g"], s0, layer, n=n, L=L, tb=cfg["hg_tb"], hb=cfg["hg_hb"],
                     chunk=cfg["hg_chunk"], sub=cfg["hg_sub"])
    x2, h2 = merge(ya.reshape(n * L, d), yb.reshape(n * L, d), rest, lw["w_out"], x1,
                   lw["mix_post"], lw["ca_pre"], tm=cfg["tm_merge"])

    (qc,) = project(h2, lw["w_cq"], 0, CA_HEADS * CA_DIM, tm=tm, tn=CA_HEADS * CA_DIM,
                    emit_f32=short, emit_bf16=not short)
    oc = cross_attend(qc.reshape(n, L, -1), mem_k, mem_v, layer, nb=cfg["ca_nb"], tq=cfg["ca_tq"])
    x3, _ = outproj(oc.reshape(n * L, -1), lw["w_co"], x2, lw["ca_post"], lw["ffn2_pre"], tm=cfg["tm_merge"])

    x4, _ = ffn_half(x3, lw["ffn2_pre"], lw["ffn2_post"], lw["ffn2_pre"], lw["ffn2_w_gu"], lw["ffn2_w_down"],
                     tm=cfg["tm_ffn"], tf=cfg["tf"], emit_next=False)
    return x4, k_rows, v_rows, s_new


def kernel(x_prompt, x_sample, cache_k, cache_v, state_hgrn, cache_mem_k, cache_mem_v, page_table, mem_prompt, ffn1_pre, ffn1_post, ffn1_w_gu, ffn1_w_down, mix_pre, mix_post, w_in, w_out, lambda_q1, lambda_k1, lambda_q2, lambda_k2, subln_g, hg_norm_g, hg_lb_logits, ca_pre, ca_post, mem_norm_g, w_cq, w_ckv, w_co, ffn2_pre, ffn2_post, ffn2_w_gu, ffn2_w_down):
    n_p, seq, d = x_prompt.shape
    n_s, dec_seq, _ = x_sample.shape
    depth = ffn1_pre.shape[0]
    page = cache_k.shape[2]
    past = page_table.shape[1] * page
    mem_len = mem_prompt.shape[1]
    ca_w = CA_HEADS * CA_DIM

    lower = jnp.cumsum(jax.nn.softmax(hg_lb_logits.astype(F32), axis=0), axis=0)
    xp = x_prompt.reshape(n_p * seq, d)
    xs = x_sample.reshape(n_s * dec_seq, d)
    pos_p = jnp.arange(seq)
    pos_s = past + jnp.arange(dec_seq)

    rows_p, rows_s = n_p * seq, n_s * dec_seq
    cfg_p = dict(tm=min(1024, rows_p), tn=1024, tm_ffn=min(512, rows_p), tf=512, tm_merge=min(256, rows_p),
                 hg_tb=min(512, seq), hg_hb=4, hg_chunk=64, hg_sub=16, ca_nb=1, ca_tq=min(512, seq))
    cfg_s = dict(tm=min(1024, rows_s), tn=1024, tm_ffn=min(512, rows_s), tf=512, tm_merge=min(256, rows_s),
                 hg_tb=dec_seq, hg_hb=8, hg_chunk=BF16_ROWS, hg_sub=BF16_ROWS, ca_nb=min(4, n_s), ca_tq=dec_seq)

    outs = [[] for _ in range(8)]
    for l in range(depth):
        lam_init = 0.8 - 0.6 * math.exp(-0.3 * l)
        lw = {
            "ffn1_pre": _row(ffn1_pre[l]), "ffn1_post": _row(ffn1_post[l]),
            "ffn1_w_gu": ffn1_w_gu[l].astype(BF16), "ffn1_w_down": ffn1_w_down[l].astype(BF16),
            "mix_pre": _row(mix_pre[l]), "mix_post": _row(mix_post[l]),
            "w_in": w_in[l].astype(BF16), "w_out": w_out[l].astype(BF16),
            "hg_norm_g": _row(hg_norm_g[l]),
            "ca_pre": _row(ca_pre[l]), "ca_post": _row(ca_post[l]),
            "w_cq": w_cq[l].astype(BF16), "w_co": w_co[l].astype(BF16),
            "ffn2_pre": _row(ffn2_pre[l]), "ffn2_post": _row(ffn2_post[l]),
            "ffn2_w_gu": ffn2_w_gu[l].astype(BF16), "ffn2_w_down": ffn2_w_down[l].astype(BF16),
        }
        lams = [_row(lambda_q1[l]), _row(lambda_k1[l]), _row(lambda_q2[l]), _row(lambda_k2[l])]
        subln = _row(subln_g[l])
        lb = _row(lower[l])

        rows_m = n_p * mem_len
        hm = rmsnorm_bf16(mem_prompt.reshape(rows_m, d), _row(mem_norm_g[l]), tm=min(512, rows_m))
        w_ckv_l = w_ckv[l].astype(BF16)
        tm_m = min(1024, rows_m)
        mk32, mk16 = project(hm, w_ckv_l, 0, ca_w, tm=tm_m, tn=ca_w, emit_f32=True, emit_bf16=True)
        mv32, mv16 = project(hm, w_ckv_l, ca_w, ca_w, tm=tm_m, tn=ca_w, emit_f32=True, emit_bf16=True)

        attn_p = functools.partial(attn_prompt, lams=lams, subln=subln, lam_init=lam_init,
                                   tq=min(256, seq))
        xp, kp, vp, sp = _group_step(xp, pos_p, n_p, seq, l, lw, lb, attn_p, None,
                                     mk16.reshape(n_p, mem_len, ca_w), mv16.reshape(n_p, mem_len, ca_w), cfg_p)

        def attn_s(q, k, v):
            return attn_sample(page_table, q, k, v, cache_k, cache_v, l, lams, subln, lam_init)

        xs, ks, vs, ss = _group_step(xs, pos_s, n_s, dec_seq, l, lw, lb, attn_s, state_hgrn,
                                     cache_mem_k, cache_mem_v, cfg_s)

        for lst, val in zip(outs, (
                kp, vp, sp, mk32.reshape(n_p, mem_len, CA_HEADS, CA_DIM),
                mv32.reshape(n_p, mem_len, CA_HEADS, CA_DIM), ks, vs, ss)):
            lst.append(val)

    return (xp.reshape(n_p, seq, d), xs.reshape(n_s, dec_seq, d), *[jnp.stack(o) for o in outs])
```

```python
import functools
import math

import jax
import jax.numpy as jnp
from jax import lax
from jax.experimental import pallas as pl
from jax.experimental.pallas import tpu as pltpu

F32 = jnp.float32
BF16 = jnp.bfloat16

EPS = 1e-6
ROPE_THETA = 500000.0
LANES = 128
BF16_ROWS = 16
MXU_COLS = 256
VMEM_LIMIT_BYTES = 56 * 1024 * 1024
LOG2E = math.log2(math.e)

DA_HEADS = 16
DA_KV_HEADS = 8
DA_DIM = 64
ROPE_DIM = DA_DIM // 4
HG_HEADS = 16
HG_DK = 128
CA_HEADS = 4
CA_DIM = 128


def _cparams(*sem):
    return pltpu.CompilerParams(dimension_semantics=sem, vmem_limit_bytes=VMEM_LIMIT_BYTES)


def _rms(x, g):
    ms = jnp.mean(x * x, axis=-1, keepdims=True)
    return (x * lax.rsqrt(ms + EPS)) * g


def _silu(x):
    return x * jax.nn.sigmoid(x)


def _pad_rows(x, rows):
    if x.shape[0] == rows:
        return x
    return jnp.concatenate([x, jnp.zeros((rows - x.shape[0], x.shape[1]), x.dtype)], axis=0)


_NT = (((1,), (1,)), ((), ()))
_TN = (((0,), (0,)), ((), ()))


def _rmsnorm_kernel(x_ref, g_ref, o_ref):
    o_ref[...] = _rms(x_ref[...], g_ref[...]).astype(o_ref.dtype)


def rmsnorm_bf16(x, g, *, tm):
    rows, d = x.shape
    return pl.pallas_call(
        _rmsnorm_kernel,
        out_shape=jax.ShapeDtypeStruct((rows, d), BF16),
        grid=(rows // tm,),
        in_specs=[pl.BlockSpec((tm, d), lambda i: (i, 0)),
                  pl.BlockSpec((1, d), lambda i: (0, 0))],
        out_specs=pl.BlockSpec((tm, d), lambda i: (i, 0)),
        compiler_params=_cparams("parallel"),
        name="rmsnorm",
    )(x, g)


def _ffn_kernel(x_ref, pre_ref, post_ref, nxt_ref, wg_ref, wu_ref, wd_ref, *rest, emit_next):
    if emit_next:
        y_ref, hn_ref, h_scr, acc_scr = rest
    else:
        y_ref, h_scr, acc_scr = rest
    j = pl.program_id(1)

    @pl.when(j == 0)
    def _():
        h_scr[...] = _rms(x_ref[...], pre_ref[...]).astype(BF16)
        acc_scr[...] = jnp.zeros(acc_scr.shape, F32)

    h = h_scr[...]
    g = jnp.dot(h, wg_ref[...], preferred_element_type=F32)
    u = jnp.dot(h, wu_ref[...], preferred_element_type=F32)
    a = (_silu(g) * u).astype(BF16)
    cw = 2 * MXU_COLS
    for c in range(acc_scr.shape[1] // cw):
        sl = slice(c * cw, (c + 1) * cw)
        acc_scr[:, sl] += jnp.dot(a, wd_ref[:, sl], preferred_element_type=F32)

    @pl.when(j == pl.num_programs(1) - 1)
    def _():
        y = x_ref[...] + 0.5 * _rms(acc_scr[...], post_ref[...])
        y_ref[...] = y
        if emit_next:
            hn_ref[...] = _rms(y, nxt_ref[...]).astype(BF16)


def ffn_half(x, pre, post, nxt, w_gu, w_down, *, tm, tf, emit_next):
    rows, d = x.shape
    d_ff = w_down.shape[0]
    nj = d_ff // tf
    out_shape = [jax.ShapeDtypeStruct((rows, d), F32)]
    out_specs = [pl.BlockSpec((tm, d), lambda i, j: (i, 0))]
    if emit_next:
        out_shape.append(jax.ShapeDtypeStruct((rows, d), BF16))
        out_specs.append(pl.BlockSpec((tm, d), lambda i, j: (i, 0)))
    vec = pl.BlockSpec((1, d), lambda i, j: (0, 0))
    res = pl.pallas_call(
        functools.partial(_ffn_kernel, emit_next=emit_next),
        out_shape=out_shape,
        grid=(rows // tm, nj),
        in_specs=[pl.BlockSpec((tm, d), lambda i, j: (i, 0)), vec, vec, vec,
                  pl.BlockSpec((d, tf), lambda i, j: (0, j)),
                  pl.BlockSpec((d, tf), lambda i, j: (0, j + nj)),
                  pl.BlockSpec((tf, d), lambda i, j: (j, 0))],
        out_specs=out_specs,
        scratch_shapes=[pltpu.VMEM((tm, d), BF16), pltpu.VMEM((tm, d), F32)],
        compiler_params=_cparams("parallel", "arbitrary"),
        name="ffn_half",
    )(x, pre, post, nxt, w_gu, w_gu, w_down)
    return res if emit_next else (res[0], None)


def _proj_kernel(h_ref, w_ref, *rest, rope, scale, emit_f32, emit_bf16):
    if rope:
        cos_ref, sa_ref, sb_ref = rest[:3]
        outs = rest[3:]
    else:
        outs = rest
    y = jnp.dot(h_ref[...], w_ref[...], preferred_element_type=F32)
    tn = y.shape[1]

    def emit(val, sl):
        if scale != 1.0:
            val = val * scale
        k = 0
        if emit_f32:
            outs[k][:, sl] = val
            k += 1
        if emit_bf16:
            outs[k][:, sl] = val.astype(BF16)

    if rope:
        cos, sa, sb = cos_ref[...], sa_ref[...], sb_ref[...]
        for gidx in range(tn // LANES):
            sl = slice(gidx * LANES, (gidx + 1) * LANES)
            yg = y[:, sl]
            up = pltpu.roll(yg, LANES - ROPE_DIM // 2, axis=1)
            dn = pltpu.roll(yg, ROPE_DIM // 2, axis=1)
            emit(yg * cos + up * sa + dn * sb, sl)
    else:
        emit(y, slice(None))


def project(h, w, c0, n, *, tm, tn, rope_tabs=None, rope_period=1, scale=1.0,
            emit_f32=True, emit_bf16=False):
    rows, kdim = h.shape
    cb0 = c0 // tn
    rope = rope_tabs is not None
    in_specs = [pl.BlockSpec((tm, kdim), lambda i, j: (i, 0)),
                pl.BlockSpec((kdim, tn), lambda i, j: (0, cb0 + j))]
    args = [h, w]
    if rope:
        tab = pl.BlockSpec((tm, LANES), lambda i, j: (i % rope_period, 0))
        in_specs += [tab, tab, tab]
        args += list(rope_tabs)
    out_shape, out_specs = [], []
    for flag, dt in ((emit_f32, F32), (emit_bf16, BF16)):
        if flag:
            out_shape.append(jax.ShapeDtypeStruct((rows, n), dt))
            out_specs.append(pl.BlockSpec((tm, tn), lambda i, j: (i, j)))
    return pl.pallas_call(
        functools.partial(_proj_kernel, rope=rope, scale=scale, emit_f32=emit_f32, emit_bf16=emit_bf16),
        out_shape=out_shape,
        grid=(rows // tm, n // tn),
        in_specs=in_specs,
        out_specs=out_specs,
        compiler_params=_cparams("parallel", "arbitrary"),
        name="project",
    )(*args)


def rope_tables(pos):
    half = ROPE_DIM // 2
    inv = jnp.float32(ROPE_THETA) ** (-jnp.arange(0, ROPE_DIM, 2, dtype=F32) / ROPE_DIM)
    ang = pos.astype(F32)[:, None] * inv[None, :]
    cos, sin = jnp.cos(ang), jnp.sin(ang)
    r = jnp.arange(LANES) % DA_DIM
    idx = r % half
    lo = (r < half)[None, :]
    hi = ((r >= half) & (r < ROPE_DIM))[None, :]
    cos_t = jnp.where(lo | hi, cos[:, idx], 1.0)
    sa_t = jnp.where(lo, -sin[:, idx], 0.0)
    sb_t = jnp.where(hi, sin[:, idx], 0.0)
    return cos_t, sa_t, sb_t


def _lambda(lq1, lk1, lq2, lk2, lam_init):
    s1 = jnp.sum(lq1 * lk1, axis=-1, keepdims=True)
    s2 = jnp.sum(lq2 * lk2, axis=-1, keepdims=True)
    return jnp.exp(s1) - jnp.exp(s2) + lam_init


def _stack_q(q0, q1):
    q0 = q0.astype(F32)
    q1 = q1.astype(F32)
    lane = lax.broadcasted_iota(jnp.int32, q0.shape, 1)
    first = lane < DA_DIM
    z = jnp.zeros_like(q0)
    return jnp.concatenate([jnp.where(first, q0, z), jnp.where(first, q1, z),
                            jnp.where(first, z, q0), jnp.where(first, z, q1)], axis=0).astype(BF16)


def _lane_tiles(x):
    return [x[:, j * LANES:(j + 1) * LANES] for j in range(x.shape[1] // LANES)]


def _diff_finish(acc, l, lam, t, subln, out_scale):
    o = acc / l
    outs = []
    for g in range(2):
        d = o[g * t:(g + 1) * t] - lam * o[(2 + g) * t:(3 + g) * t]
        outs.append(_rms(d, subln) * out_scale)
    return outs


def _attn_prompt_kernel(q_ref, k_ref, v_ref, lq1, lk1, lq2, lk2, sg_ref, o_ref,
                        qs_scr, s_scr, m_scr, acc_scr, *, tq, kvh, lam_init):
    qi = pl.program_id(1)
    tk = 2 * tq
    n_full = qi // 2
    lam = _lambda(lq1[...], lk1[...], lq2[...], lk2[...], lam_init)
    row = lax.broadcasted_iota(jnp.int32, (4 * tq, tk), 0) % tq + qi * tq
    col = lax.broadcasted_iota(jnp.int32, (4 * tq, tk), 1) + n_full * tk
    visible = col <= row

    def head_lanes(h):
        return slice(h * LANES, (h + 1) * LANES)

    def lane_max(s):
        return functools.reduce(jnp.maximum, _lane_tiles(s))

    def start(h):
        qs_scr[h % 2] = _stack_q(q_ref[0, :, head_lanes(2 * h)], q_ref[0, :, head_lanes(2 * h + 1)])
        m_scr[h % 2] = jnp.full(m_scr.shape[1:], -jnp.inf, F32)

    def scores(h, kp):
        off = pl.multiple_of(kp * tk, tk)
        return lax.dot_general(qs_scr[h % 2], k_ref[0, pl.ds(off, tk), head_lanes(h)], _NT,
                               preferred_element_type=F32)

    def pass1(h, kp):
        s = scores(h, kp)
        s_scr[h % 2, kp] = s
        m_scr[h % 2] = jnp.maximum(m_scr[h % 2], lane_max(s))

    def diag1(h):
        s = jnp.where(visible, scores(h, n_full), -jnp.inf)
        s_scr[h % 2, n_full] = s
        m_row = jnp.max(jnp.maximum(m_scr[h % 2], lane_max(s)), axis=-1, keepdims=True)
        m_scr[h % 2] = jnp.broadcast_to(m_row, m_scr.shape[1:])

    def accumulate(h, s, v):
        m_b = m_scr[h % 2]
        p = jnp.concatenate([jnp.exp2(st - m_b) for st in _lane_tiles(s)], axis=1).astype(BF16)
        v_t = jnp.concatenate([v.astype(F32).T, jnp.ones((BF16_ROWS, v.shape[0]), F32)], axis=0).astype(BF16)
        acc_scr[...] += lax.dot_general(v_t, p, _NT, preferred_element_type=F32)

    def reset2():
        acc_scr[...] = jnp.zeros(acc_scr.shape, F32)

    def pass2(h, kp):
        off = pl.multiple_of(kp * tk, tk)
        accumulate(h, s_scr[h % 2, kp], v_ref[0, pl.ds(off, tk), head_lanes(h)])

    def diag2(h, between=None):
        off = pl.multiple_of(n_full * tk, tk)
        accumulate(h, s_scr[h % 2, n_full, :, :tq], v_ref[0, pl.ds(off, tq), head_lanes(h)])
        if between is not None:
            between()

        @pl.when(qi % 2 == 1)
        def _():
            accumulate(h, s_scr[h % 2, n_full, :, tq:],
                       v_ref[0, pl.ds(pl.multiple_of(off + tq, tq), tq), head_lanes(h)])

        acc = acc_scr[...]
        o_t = acc[:LANES] / acc[LANES:LANES + 1]
        for g in range(2):
            d = o_t[:, g * tq:(g + 1) * tq] - lam * o_t[:, (2 + g) * tq:(3 + g) * tq]
            ms = jnp.mean(d * d, axis=0, keepdims=True)
            y = (d * lax.rsqrt(ms + EPS)) * sg_ref[...] * (1.0 - lam_init)
            o_ref[0, :, head_lanes(2 * h + g)] = y.T.astype(o_ref.dtype)

    def loop(fn):
        def body(kp, carry):
            fn(kp)
            return carry
        lax.fori_loop(0, n_full, body, 0)

    start(0)
    loop(functools.partial(pass1, 0))
    diag1(0)
    for h in range(1, kvh):
        start(h)
        reset2()

        def both(kp, h=h):
            pass1(h, kp)
            pass2(h - 1, kp)

        loop(both)
        diag2(h - 1, between=functools.partial(diag1, h))
    reset2()
    loop(functools.partial(pass2, kvh - 1))
    diag2(kvh - 1)


def attn_prompt(q, k, v, lams, subln, lam_init, *, tq):
    n, L, dq = q.shape
    dk = k.shape[2]
    kvh = dk // LANES
    vec = pl.BlockSpec((1, DA_DIM), lambda b, i: (0, 0))
    subln = subln.reshape(LANES, 1)
    return pl.pallas_call(
        functools.partial(_attn_prompt_kernel, tq=tq, kvh=kvh, lam_init=lam_init),
        out_shape=jax.ShapeDtypeStruct(q.shape, BF16),
        grid=(n, L // tq),
        in_specs=[pl.BlockSpec((1, tq, dq), lambda b, i: (b, i, 0)),
                  pl.BlockSpec((1, L, dk), lambda b, i: (b, 0, 0)),
                  pl.BlockSpec((1, L, dk), lambda b, i: (b, 0, 0)),
                  vec, vec, vec, vec,
                  pl.BlockSpec((LANES, 1), lambda b, i: (0, 0))],
        out_specs=pl.BlockSpec((1, tq, dq), lambda b, i: (b, i, 0)),
        scratch_shapes=[pltpu.VMEM((2, 4 * tq, LANES), BF16),
                        pltpu.VMEM((2, L // (2 * tq), 4 * tq, 2 * tq), F32),
                        pltpu.VMEM((2, 4 * tq, LANES), F32),
                        pltpu.VMEM((LANES + BF16_ROWS, 4 * tq), F32)],
        compiler_params=_cparams("parallel", "arbitrary"),
        name="attn_prompt",
    )(q, k, v, *lams, subln)


def _attn_sample_kernel(pt_ref, q_ref, kn_ref, vn_ref, *rest, n_pages, t, kvh, lam_init):
    kp = rest[:n_pages]
    vp = rest[n_pages:2 * n_pages]
    lq1, lk1, lq2, lk2, sg_ref, o_ref, qs_scr, o_scr = rest[2 * n_pages:]
    r = 4 * t
    page = kp[0].shape[2] // kvh
    for h in range(kvh):
        q0 = q_ref[0, :, (2 * h) * LANES:(2 * h + 1) * LANES]
        q1 = q_ref[0, :, (2 * h + 1) * LANES:(2 * h + 2) * LANES]
        qs_scr[h * r:(h + 1) * r, :] = _stack_q(q0, q1)
    lam = _lambda(lq1[...], lk1[...], lq2[...], lk2[...], lam_init)
    row = lax.broadcasted_iota(jnp.int32, (r, LANES), 0) % t
    col = lax.broadcasted_iota(jnp.int32, (r, LANES), 1)
    new_visible = col <= row

    def head(h, carry):
        def rows_of(ref, lead, n_tok):
            return ref[lead + (pl.ds(h, n_tok, stride=kvh), slice(None))]

        qs = qs_scr[pl.ds(pl.multiple_of(h * r, r), r), :]
        kn = _pad_rows(rows_of(kn_ref, (0,), t), LANES).astype(BF16)
        vn = _pad_rows(rows_of(vn_ref, (0,), t), LANES).astype(BF16)
        tiles = [jnp.dot(qs, rows_of(kp[pg], (0, 0), page).T.astype(BF16),
                         preferred_element_type=F32) for pg in range(n_pages)]
        tiles.append(jnp.where(new_visible, lax.dot_general(qs, kn, _NT, preferred_element_type=F32),
                               -jnp.inf))
        m = jnp.max(functools.reduce(jnp.maximum, tiles), axis=-1, keepdims=True)
        ps = [jnp.exp2(st - m) for st in tiles]
        l = jnp.sum(functools.reduce(jnp.add, ps), axis=-1, keepdims=True)
        acc = jnp.dot(ps[n_pages].astype(BF16), vn, preferred_element_type=F32)
        for pg in range(n_pages):
            acc = acc + jnp.dot(ps[pg].astype(BF16), rows_of(vp[pg], (0, 0), page).astype(BF16),
                                preferred_element_type=F32)
        o0, o1 = _diff_finish(acc, l, lam, t, sg_ref[...], 1.0 - lam_init)
        o_scr[h] = jnp.concatenate([o0, o1], axis=1)
        return carry

    lax.fori_loop(0, kvh, head, 0, unroll=4)
    for h in range(kvh):
        o_ref[0, :, 2 * h * LANES:(2 * h + 2) * LANES] = o_scr[h]


def attn_sample(page_table, q, k_new, v_new, cache_k, cache_v, layer, lams, subln, lam_init):
    n, t, _ = q.shape
    kvh = k_new.shape[2]
    n_pages = page_table.shape[1]
    depth, n_pool, page = cache_k.shape[:3]
    k_new, v_new = (a.reshape(n, t * kvh, LANES) for a in (k_new, v_new))
    cache_k, cache_v = (a.reshape(depth, n_pool, page * kvh, LANES) for a in (cache_k, cache_v))

    def page_spec(pg):
        return pl.BlockSpec((1, 1, page * kvh, LANES), lambda b, pt: (layer, pt[b, pg], 0, 0))

    new_spec = pl.BlockSpec((1, t * kvh, LANES), lambda b, pt: (b, 0, 0))
    q_spec = pl.BlockSpec((1, t, q.shape[2]), lambda b, pt: (b, 0, 0))
    vec = pl.BlockSpec((1, DA_DIM), lambda b, pt: (0, 0))
    pages = [page_spec(pg) for pg in range(n_pages)]
    grid_spec = pltpu.PrefetchScalarGridSpec(
        num_scalar_prefetch=1,
        grid=(n,),
        in_specs=[q_spec, new_spec, new_spec] + pages + pages
                 + [vec, vec, vec, vec, pl.BlockSpec((1, LANES), lambda b, pt: (0, 0))],
        out_specs=q_spec,
        scratch_shapes=[pltpu.VMEM((kvh * 4 * t, LANES), BF16), pltpu.VMEM((kvh, t, 2 * LANES), F32)],
    )
    return pl.pallas_call(
        functools.partial(_attn_sample_kernel, n_pages=n_pages, t=t, kvh=kvh, lam_init=lam_init),
        out_shape=jax.ShapeDtypeStruct(q.shape, F32),
        grid_spec=grid_spec,
        compiler_params=_cparams("parallel"),
        name="attn_sample",
    )(page_table, q, k_new, v_new, *([cache_k] * n_pages), *([cache_v] * n_pages), *lams, subln)


def _split3(x):
    hi = x.astype(BF16)
    r1 = x - hi.astype(F32)
    mid = r1.astype(BF16)
    lo = (r1 - mid.astype(F32)).astype(BF16)
    return hi, mid, lo


def _hgrn_chunk(qb, fb, vb, lb, states, group_ones, off_mask, f_scr, k_scr, v_scr, *, sub, valid):
    c_len, w = qb.shape
    hb = w // LANES
    nsub = c_len // sub
    gw = group_ones.shape[0]
    heads = [slice(h * LANES, (h + 1) * LANES) for h in range(hb)]
    f = lb + (1.0 - lb) * jax.nn.sigmoid(fb)
    if valid < c_len:
        f = jnp.where(lax.broadcasted_iota(jnp.int32, (c_len, w), 0) < valid, f, 1.0)
    lf = jnp.log(f)
    kk = 1.0 - f
    qq = _silu(qb)
    vb16 = vb.astype(BF16)
    for h, sl in enumerate(heads):
        f_scr[h] = f[:, sl]
        k_scr[h] = kk[:, sl]
        v_scr[h] = vb[:, sl]

    def row_bcast(ref, r):
        return jnp.concatenate([jnp.broadcast_to(ref[h, pl.ds(r, 1), :], (sub, LANES)) for h in range(hb)],
                               axis=1)

    r_i = lax.broadcasted_iota(jnp.int32, (c_len, c_len), 0)
    c_i = lax.broadcasted_iota(jnp.int32, (c_len, c_len), 1)
    tri = (c_i <= r_i).astype(BF16)
    b = functools.reduce(jnp.add, [jnp.dot(tri, part, preferred_element_type=F32) for part in _split3(lf)])
    b_last = b[c_len - 1:c_len, :]
    qe = (qq * jnp.exp(b)).astype(BF16)
    khat = (kk * jnp.exp(b_last - b)).astype(BF16)
    e_last = jnp.exp(b_last)

    if nsub > 1:
        q_parts, k_parts, v_parts = [], [], []
        for i in range(1, nsub):
            lo = i * sub
            b_ref = b[lo - 1:lo, :]
            q_parts.append((qq[lo:lo + sub] * jnp.exp(b[lo:lo + sub] - b_ref)).astype(BF16))
            k_parts.append((kk[:lo] * jnp.exp(b_ref - b[:lo])).astype(BF16))
            v_parts.append(vb16[:lo])
        q_off, k_off, v_off = (jnp.concatenate(p, axis=0) for p in (q_parts, k_parts, v_parts))
        atts = [lax.dot_general(q_off[:, sl], k_off[:, sl], _NT, preferred_element_type=F32) for sl in heads]

    o_inter = [jnp.dot(qe[:, sl], states[h].astype(BF16), preferred_element_type=F32)
               for h, sl in enumerate(heads)]
    upds = [lax.dot_general(khat[:, sl], vb16[:, sl], _TN, preferred_element_type=F32) for sl in heads]

    trow = lax.broadcasted_iota(jnp.int32, (sub, w), 0)
    o_diag = []
    for i in range(nsub):
        lo = i * sub
        q_i = qq[lo:lo + sub]
        g = jnp.where(trow == sub - 1, q_i, 0.0)
        parts = [None] * sub
        parts[sub - 1] = g * row_bcast(k_scr, lo + sub - 1)
        for s in range(sub - 2, -1, -1):
            g = jnp.where(trow == s, q_i, g * row_bcast(f_scr, lo + s + 1))
            parts[s] = g * row_bcast(k_scr, lo + s)
        p_all = jnp.concatenate(parts, axis=0).astype(BF16)
        rsum = jnp.concatenate(
            [jnp.dot(p_all[:, j * gw:(j + 1) * gw], group_ones, preferred_element_type=F32)
             for j in range(w // gw)], axis=1)
        o_i = rsum[0:sub] * row_bcast(v_scr, lo)
        for s in range(1, sub):
            o_i = o_i + rsum[s * sub:(s + 1) * sub] * row_bcast(v_scr, lo + s)
        o_diag.append(o_i)
    o_diag = jnp.concatenate(o_diag, axis=0) if nsub > 1 else o_diag[0]

    outs, new_states = [], []
    for h, sl in enumerate(heads):
        o_h = o_inter[h] + o_diag[:, sl]
        if nsub > 1:
            att = jnp.where(off_mask, atts[h], 0.0).astype(BF16)
            o_off = jnp.dot(att, v_off[:, sl], preferred_element_type=F32)
            o_h = o_h + jnp.concatenate([jnp.zeros((sub, LANES), F32), o_off], axis=0)
        outs.append(o_h)
        decay = jnp.broadcast_to(e_last[:, sl], (LANES, LANES)).T
        new_states.append(decay * states[h] + upds[h])
    return outs, new_states


def _hgrn_kernel(q_ref, f_ref, v_ref, lb_ref, g_ref, *rest, hb, chunk, sub, has_init):
    if has_init:
        s0_ref, o_ref, s_ref, f_scr, k_scr, v_scr = rest
    else:
        o_ref, s_ref, f_scr, k_scr, v_scr = rest
    tb = pl.program_id(2)

    @pl.when(tb == 0)
    def _():
        if has_init:
            s_ref[...] = s0_ref[...]
        else:
            s_ref[...] = jnp.zeros(s_ref.shape, F32)

    gw = MXU_COLS if (hb * LANES) % MXU_COLS == 0 else LANES
    gr = lax.broadcasted_iota(jnp.int32, (gw, gw), 0) // LANES
    gc = lax.broadcasted_iota(jnp.int32, (gw, gw), 1) // LANES
    group_ones = (gr == gc).astype(BF16)
    nsub = chunk // sub
    off_mask = None
    if nsub > 1:
        shape = ((nsub - 1) * sub, sub * nsub * (nsub - 1) // 2)
        qblk = lax.broadcasted_iota(jnp.int32, shape, 0) // sub + 1
        kcol = lax.broadcasted_iota(jnp.int32, shape, 1)
        off_mask = functools.reduce(jnp.logical_or, [
            (qblk == i) & (kcol >= sub * i * (i - 1) // 2) & (kcol < sub * i * (i + 1) // 2)
            for i in range(1, nsub)])
    t_blk = q_ref.shape[1]

    def run(rows, valid):
        tile = lambda ref: _pad_rows(ref[0, rows, :], chunk)
        outs, new = _hgrn_chunk(tile(q_ref), tile(f_ref), tile(v_ref), lb_ref[...],
                                [s_ref[0, hh] for hh in range(hb)], group_ones, off_mask,
                                f_scr, k_scr, v_scr, sub=sub, valid=valid)
        for hh in range(hb):
            s_ref[0, hh] = new[hh]
            o_ref[0, rows, hh * LANES:(hh + 1) * LANES] = _rms(outs[hh][:valid], g_ref[...])

    if t_blk < chunk:
        run(slice(None), t_blk)
        return

    def body(ci, carry):
        run(pl.ds(pl.multiple_of(ci * chunk, chunk), chunk), chunk)
        return carry

    lax.fori_loop(0, t_blk // chunk, body, 0)


def hgrn(rest, lb, hg_g, s0, layer, *, n, L, tb, hb, chunk, sub):
    d = HG_HEADS * HG_DK
    x3 = rest.reshape(n, L, rest.shape[1])
    nhb = HG_HEADS // hb
    w = hb * LANES
    has_init = s0 is not None

    def col_spec(seg):
        return pl.BlockSpec((1, tb, w), lambda b, h, t: (b, t, seg * (d // w) + h))

    in_specs = [col_spec(0), col_spec(1), col_spec(2),
                pl.BlockSpec((1, w), lambda b, h, t: (0, h)),
                pl.BlockSpec((1, LANES), lambda b, h, t: (0, 0))]
    args = [x3, x3, x3, lb, hg_g]
    st_spec = pl.BlockSpec((1, hb, HG_DK, LANES), lambda b, h, t: (b, h, 0, 0))
    if has_init:
        in_specs.append(pl.BlockSpec((1, 1, hb, HG_DK, LANES), lambda b, h, t: (layer, b, h, 0, 0)))
        args.append(s0)
    kernel_fn = functools.partial(_hgrn_kernel, hb=hb, chunk=chunk, sub=sub, has_init=has_init)
    if has_init:
        inner = kernel_fn

        def kernel_fn(q_ref, f_ref, v_ref, lb_ref, g_ref, s0_ref, *rest):
            inner(q_ref, f_ref, v_ref, lb_ref, g_ref, s0_ref.at[0], *rest)

    return pl.pallas_call(
        kernel_fn,
        out_shape=[jax.ShapeDtypeStruct((n, L, d), F32),
                   jax.ShapeDtypeStruct((n, HG_HEADS, HG_DK, LANES), F32)],
        grid=(n, nhb, L // tb),
        in_specs=in_specs,
        out_specs=[pl.BlockSpec((1, tb, w), lambda b, h, t: (b, t, h)), st_spec],
        scratch_shapes=[pltpu.VMEM((hb, chunk, LANES), F32)] * 3,
        compiler_params=_cparams("parallel", "parallel", "arbitrary"),
        name="hgrn",
    )(*args)


def _out_tail(a, w_ref, x_ref, post_ref, nxt_ref, y_ref, hn_ref):
    t = jnp.dot(a, w_ref[...], preferred_element_type=F32)
    y = x_ref[...] + _rms(t, post_ref[...])
    y_ref[...] = y
    hn_ref[...] = _rms(y, nxt_ref[...]).astype(BF16)


def _outproj_kernel(a_ref, w_ref, x_ref, post_ref, nxt_ref, y_ref, hn_ref):
    _out_tail(a_ref[...].astype(BF16), w_ref, x_ref, post_ref, nxt_ref, y_ref, hn_ref)


def _merge_kernel(ya_ref, yb_ref, go_ref, ga_ref, gb_ref, w_ref, x_ref, post_ref, nxt_ref, y_ref, hn_ref):
    m = (jax.nn.sigmoid(ga_ref[...]) * ya_ref[...].astype(F32)
         + jax.nn.sigmoid(gb_ref[...]) * (yb_ref[...] * _silu(go_ref[...])))
    _out_tail(m.astype(BF16), w_ref, x_ref, post_ref, nxt_ref, y_ref, hn_ref)


def _out_common(kernel, lead_args, lead_specs, w, x, post, nxt, tm, name):
    rows, d = x.shape
    kdim = w.shape[0]
    vec = pl.BlockSpec((1, d), lambda i: (0, 0))
    row = pl.BlockSpec((tm, d), lambda i: (i, 0))
    return pl.pallas_call(
        kernel,
        out_shape=[jax.ShapeDtypeStruct((rows, d), F32), jax.ShapeDtypeStruct((rows, d), BF16)],
        grid=(rows // tm,),
        in_specs=lead_specs + [pl.BlockSpec((kdim, d), lambda i: (0, 0)), row, vec, vec],
        out_specs=[row, row],
        compiler_params=_cparams("parallel"),
        name=name,
    )(*lead_args, w, x, post, nxt)


def outproj(a, w, x, post, nxt, *, tm):
    spec = pl.BlockSpec((tm, a.shape[1]), lambda i: (i, 0))
    return _out_common(_outproj_kernel, [a], [spec], w, x, post, nxt, tm, "outproj")


def merge(ya, yb, rest, w, x, post, nxt, *, tm):
    d = x.shape[1]
    row = pl.BlockSpec((tm, d), lambda i: (i, 0))
    gate = lambda seg: pl.BlockSpec((tm, d), lambda i: (i, seg))
    return _out_common(_merge_kernel, [ya, yb, rest, rest, rest], [row, row, gate(3), gate(4), gate(5)],
                       w, x, post, nxt, tm, "merge")


def _cross_kernel(q_ref, k_ref, v_ref, o_ref, *, nb, heads, native):
    scale = CA_DIM ** -0.5 * LOG2E
    for b in range(nb):
        for h in range(heads):
            sl = slice(h * CA_DIM, (h + 1) * CA_DIM)
            q = q_ref[b, :, sl]
            tq = q.shape[0]
            if tq % BF16_ROWS:
                q = _pad_rows(q.astype(F32), BF16_ROWS * pl.cdiv(tq, BF16_ROWS))
            if native:
                k, v = k_ref[0, b, :, h, :], v_ref[0, b, :, h, :]
            else:
                k, v = k_ref[b, :, sl], v_ref[b, :, sl]
            s = lax.dot_general(q.astype(BF16), k.astype(BF16), _NT, preferred_element_type=F32) * scale
            p = jnp.exp2(s - jnp.max(s, axis=-1, keepdims=True))
            l = jnp.sum(p, axis=-1, keepdims=True)
            o = jnp.dot(p.astype(BF16), v.astype(BF16), preferred_element_type=F32) / l
            o_ref[b, :, sl] = o[:tq].astype(o_ref.dtype)


def cross_attend(q, mk, mv, layer, *, nb, tq):
    n, L, w = q.shape
    native = mk.ndim == 5
    qspec = pl.BlockSpec((nb, tq, w), lambda b, i: (b, i, 0))
    if native:
        kspec = pl.BlockSpec((1, nb) + mk.shape[2:], lambda b, i: (layer, b, 0, 0, 0))
    else:
        kspec = pl.BlockSpec((nb,) + mk.shape[1:], lambda b, i: (b, 0, 0))
    return pl.pallas_call(
        functools.partial(_cross_kernel, nb=nb, heads=w // CA_DIM, native=native),
        out_shape=jax.ShapeDtypeStruct(q.shape, q.dtype),
        grid=(n // nb, L // tq),
        in_specs=[qspec, kspec, kspec],
        out_specs=qspec,
        compiler_params=_cparams("parallel", "arbitrary"),
        name="cross_attend",
    )(q, mk, mv)


def _row(v):
    return v.reshape(1, -1).astype(F32)


def _group_step(x, pos, n, L, layer, lw, lower, attn_fn, s0, mem_k, mem_v, cfg):
    d = x.shape[1]
    tm = cfg["tm"]
    x1, h1 = ffn_half(x, lw["ffn1_pre"], lw["ffn1_post"], lw["mix_pre"], lw["ffn1_w_gu"], lw["ffn1_w_down"],
                      tm=cfg["tm_ffn"], tf=cfg["tf"], emit_next=True)

    tabs = rope_tables(pos)
    period = max(L // tm, 1)
    if L < tm:
        tabs = tuple(jnp.tile(t, (tm // L, 1)) for t in tabs)
    nq = DA_HEADS * 2 * DA_DIM
    nk = DA_KV_HEADS * 2 * DA_DIM
    w_in = lw["w_in"]
    short = L % BF16_ROWS != 0
    (qa,) = project(h1, w_in, 0, nq, tm=tm, tn=cfg["tn"], rope_tabs=tabs, rope_period=period,
                    scale=DA_DIM ** -0.5 * LOG2E, emit_f32=short, emit_bf16=not short)
    kv_out = project(h1, w_in, nq, nk, tm=tm, tn=cfg["tn"], rope_tabs=tabs, rope_period=period,
                     emit_f32=True, emit_bf16=not short)
    vv_out = project(h1, w_in, nq + nk, nk, tm=tm, tn=cfg["tn"], emit_f32=True, emit_bf16=not short)
    k_rows = kv_out[0].reshape(n, L, DA_KV_HEADS, 2 * DA_DIM)
    v_rows = vv_out[0].reshape(n, L, DA_KV_HEADS, 2 * DA_DIM)
    (rest,) = project(h1, w_in, nq + 2 * nk, 6 * d, tm=tm, tn=cfg["tn"], emit_f32=True, emit_bf16=False)

    if short:
        ya = attn_fn(qa.reshape(n, L, nq), k_rows, v_rows)
    else:
        ya = attn_fn(qa.reshape(n, L, nq), kv_out[1].reshape(n, L, nk), vv_out[1].reshape(n, L, nk))
    yb, s_new = hgrn(rest, lower, lw["hg_norm_g"], s0, layer, n=n, L=L, tb=cfg["hg_tb"], hb=cfg["hg_hb"],
                     chunk=cfg["hg_chunk"], sub=cfg["hg_sub"])
    x2, h2 = merge(ya.reshape(n * L, d), yb.reshape(n * L, d), rest, lw["w_out"], x1,
                   lw["mix_post"], lw["ca_pre"], tm=cfg["tm_merge"])

    (qc,) = project(h2, lw["w_cq"], 0, CA_HEADS * CA_DIM, tm=tm, tn=CA_HEADS * CA_DIM,
                    emit_f32=short, emit_bf16=not short)
    oc = cross_attend(qc.reshape(n, L, -1), mem_k, mem_v, layer, nb=cfg["ca_nb"], tq=cfg["ca_tq"])
    x3, _ = outproj(oc.reshape(n * L, -1), lw["w_co"], x2, lw["ca_post"], lw["ffn2_pre"], tm=cfg["tm_merge"])

    x4, _ = ffn_half(x3, lw["ffn2_pre"], lw["ffn2_post"], lw["ffn2_pre"], lw["ffn2_w_gu"], lw["ffn2_w_down"],
                     tm=cfg["tm_ffn"], tf=cfg["tf"], emit_next=False)
    return x4, k_rows, v_rows, s_new


def kernel(x_prompt, x_sample, cache_k, cache_v, state_hgrn, cache_mem_k, cache_mem_v, page_table, mem_prompt, ffn1_pre, ffn1_post, ffn1_w_gu, ffn1_w_down, mix_pre, mix_post, w_in, w_out, lambda_q1, lambda_k1, lambda_q2, lambda_k2, subln_g, hg_norm_g, hg_lb_logits, ca_pre, ca_post, mem_norm_g, w_cq, w_ckv, w_co, ffn2_pre, ffn2_post, ffn2_w_gu, ffn2_w_down):
    n_p, seq, d = x_prompt.shape
    n_s, dec_seq, _ = x_sample.shape
    depth = ffn1_pre.shape[0]
    page = cache_k.shape[2]
    past = page_table.shape[1] * page
    mem_len = mem_prompt.shape[1]
    ca_w = CA_HEADS * CA_DIM

    lower = jnp.cumsum(jax.nn.softmax(hg_lb_logits.astype(F32), axis=0), axis=0)
    xp = x_prompt.reshape(n_p * seq, d)
    xs = x_sample.reshape(n_s * dec_seq, d)
    pos_p = jnp.arange(seq)
    pos_s = past + jnp.arange(dec_seq)

    rows_p, rows_s = n_p * seq, n_s * dec_seq
    cfg_p = dict(tm=min(1024, rows_p), tn=1024, tm_ffn=min(512, rows_p), tf=512, tm_merge=min(256, rows_p),
                 hg_tb=min(512, seq), hg_hb=4, hg_chunk=64, hg_sub=16, ca_nb=1, ca_tq=min(512, seq))
    cfg_s = dict(tm=min(1024, rows_s), tn=1024, tm_ffn=min(512, rows_s), tf=512, tm_merge=min(256, rows_s),
                 hg_tb=dec_seq, hg_hb=16, hg_chunk=BF16_ROWS, hg_sub=BF16_ROWS, ca_nb=min(4, n_s), ca_tq=dec_seq)

    outs = [[] for _ in range(8)]
    for l in range(depth):
        lam_init = 0.8 - 0.6 * math.exp(-0.3 * l)
        lw = {
            "ffn1_pre": _row(ffn1_pre[l]), "ffn1_post": _row(ffn1_post[l]),
            "ffn1_w_gu": ffn1_w_gu[l].astype(BF16), "ffn1_w_down": ffn1_w_down[l].astype(BF16),
            "mix_pre": _row(mix_pre[l]), "mix_post": _row(mix_post[l]),
            "w_in": w_in[l].astype(BF16), "w_out": w_out[l].astype(BF16),
            "hg_norm_g": _row(hg_norm_g[l]),
            "ca_pre": _row(ca_pre[l]), "ca_post": _row(ca_post[l]),
            "w_cq": w_cq[l].astype(BF16), "w_co": w_co[l].astype(BF16),
            "ffn2_pre": _row(ffn2_pre[l]), "ffn2_post": _row(ffn2_post[l]),
            "ffn2_w_gu": ffn2_w_gu[l].astype(BF16), "ffn2_w_down": ffn2_w_down[l].astype(BF16),
        }
        lams = [_row(lambda_q1[l]), _row(lambda_k1[l]), _row(lambda_q2[l]), _row(lambda_k2[l])]
        subln = _row(subln_g[l])
        lb = _row(lower[l])

        rows_m = n_p * mem_len
        hm = rmsnorm_bf16(mem_prompt.reshape(rows_m, d), _row(mem_norm_g[l]), tm=min(512, rows_m))
        w_ckv_l = w_ckv[l].astype(BF16)
        tm_m = min(1024, rows_m)
        mk32, mk16 = project(hm, w_ckv_l, 0, ca_w, tm=tm_m, tn=ca_w, emit_f32=True, emit_bf16=True)
        mv32, mv16 = project(hm, w_ckv_l, ca_w, ca_w, tm=tm_m, tn=ca_w, emit_f32=True, emit_bf16=True)

        attn_p = functools.partial(attn_prompt, lams=lams, subln=subln, lam_init=lam_init,
                                   tq=min(256, seq))
        xp, kp, vp, sp = _group_step(xp, pos_p, n_p, seq, l, lw, lb, attn_p, None,
                                     mk16.reshape(n_p, mem_len, ca_w), mv16.reshape(n_p, mem_len, ca_w), cfg_p)

        def attn_s(q, k, v):
            return attn_sample(page_table, q, k, v, cache_k, cache_v, l, lams, subln, lam_init)

        xs, ks, vs, ss = _group_step(xs, pos_s, n_s, dec_seq, l, lw, lb, attn_s, state_hgrn,
                                     cache_mem_k, cache_mem_v, cfg_s)

        for lst, val in zip(outs, (
                kp, vp, sp, mk32.reshape(n_p, mem_len, CA_HEADS, CA_DIM),
                mv32.reshape(n_p, mem_len, CA_HEADS, CA_DIM), ks, vs, ss)):
            lst.append(val)

    return (xp.reshape(n_p, seq, d), xs.reshape(n_s, dec_seq, d), *[jnp.stack(o) for o in outs])
```

```python
import functools
import math

import jax
import jax.numpy as jnp
from jax import lax
from jax.experimental import pallas as pl
from jax.experimental.pallas import tpu as pltpu

F32 = jnp.float32
BF16 = jnp.bfloat16

EPS = 1e-6
ROPE_THETA = 500000.0
LANES = 128
BF16_ROWS = 16
MXU_COLS = 256
VMEM_LIMIT_BYTES = 56 * 1024 * 1024
LOG2E = math.log2(math.e)

DA_HEADS = 16
DA_KV_HEADS = 8
DA_DIM = 64
ROPE_DIM = DA_DIM // 4
HG_HEADS = 16
HG_DK = 128
CA_HEADS = 4
CA_DIM = 128


def _cparams(*sem):
    return pltpu.CompilerParams(dimension_semantics=sem, vmem_limit_bytes=VMEM_LIMIT_BYTES)


def _rms(x, g):
    ms = jnp.mean(x * x, axis=-1, keepdims=True)
    return (x * lax.rsqrt(ms + EPS)) * g


def _silu(x):
    return x * jax.nn.sigmoid(x)


def _pad_rows(x, rows):
    if x.shape[0] == rows:
        return x
    return jnp.concatenate([x, jnp.zeros((rows - x.shape[0], x.shape[1]), x.dtype)], axis=0)


_NT = (((1,), (1,)), ((), ()))
_TN = (((0,), (0,)), ((), ()))


def _rmsnorm_kernel(x_ref, g_ref, o_ref):
    o_ref[...] = _rms(x_ref[...], g_ref[...]).astype(o_ref.dtype)


def rmsnorm_bf16(x, g, *, tm):
    rows, d = x.shape
    return pl.pallas_call(
        _rmsnorm_kernel,
        out_shape=jax.ShapeDtypeStruct((rows, d), BF16),
        grid=(rows // tm,),
        in_specs=[pl.BlockSpec((tm, d), lambda i: (i, 0)),
                  pl.BlockSpec((1, d), lambda i: (0, 0))],
        out_specs=pl.BlockSpec((tm, d), lambda i: (i, 0)),
        compiler_params=_cparams("parallel"),
        name="rmsnorm",
    )(x, g)


def _ffn_kernel(x_ref, pre_ref, post_ref, nxt_ref, wg_ref, wu_ref, wd_ref, *rest, emit_next):
    if emit_next:
        y_ref, hn_ref, h_scr, acc_scr = rest
    else:
        y_ref, h_scr, acc_scr = rest
    j = pl.program_id(1)

    @pl.when(j == 0)
    def _():
        h_scr[...] = _rms(x_ref[...], pre_ref[...]).astype(BF16)
        acc_scr[...] = jnp.zeros(acc_scr.shape, F32)

    h = h_scr[...]
    g = jnp.dot(h, wg_ref[...], preferred_element_type=F32)
    u = jnp.dot(h, wu_ref[...], preferred_element_type=F32)
    a = (_silu(g) * u).astype(BF16)
    cw = 2 * MXU_COLS
    for c in range(acc_scr.shape[1] // cw):
        sl = slice(c * cw, (c + 1) * cw)
        acc_scr[:, sl] += jnp.dot(a, wd_ref[:, sl], preferred_element_type=F32)

    @pl.when(j == pl.num_programs(1) - 1)
    def _():
        y = x_ref[...] + 0.5 * _rms(acc_scr[...], post_ref[...])
        y_ref[...] = y
        if emit_next:
            hn_ref[...] = _rms(y, nxt_ref[...]).astype(BF16)


def ffn_half(x, pre, post, nxt, w_gu, w_down, *, tm, tf, emit_next):
    rows, d = x.shape
    d_ff = w_down.shape[0]
    nj = d_ff // tf
    out_shape = [jax.ShapeDtypeStruct((rows, d), F32)]
    out_specs = [pl.BlockSpec((tm, d), lambda i, j: (i, 0))]
    if emit_next:
        out_shape.append(jax.ShapeDtypeStruct((rows, d), BF16))
        out_specs.append(pl.BlockSpec((tm, d), lambda i, j: (i, 0)))
    vec = pl.BlockSpec((1, d), lambda i, j: (0, 0))
    res = pl.pallas_call(
        functools.partial(_ffn_kernel, emit_next=emit_next),
        out_shape=out_shape,
        grid=(rows // tm, nj),
        in_specs=[pl.BlockSpec((tm, d), lambda i, j: (i, 0)), vec, vec, vec,
                  pl.BlockSpec((d, tf), lambda i, j: (0, j)),
                  pl.BlockSpec((d, tf), lambda i, j: (0, j + nj)),
                  pl.BlockSpec((tf, d), lambda i, j: (j, 0))],
        out_specs=out_specs,
        scratch_shapes=[pltpu.VMEM((tm, d), BF16), pltpu.VMEM((tm, d), F32)],
        compiler_params=_cparams("parallel", "arbitrary"),
        name="ffn_half",
    )(x, pre, post, nxt, w_gu, w_gu, w_down)
    return res if emit_next else (res[0], None)


def _proj_kernel(h_ref, w_ref, *rest, rope, scale, emit_f32, emit_bf16):
    if rope:
        cos_ref, sa_ref, sb_ref = rest[:3]
        outs = rest[3:]
    else:
        outs = rest
    y = jnp.dot(h_ref[...], w_ref[...], preferred_element_type=F32)
    tn = y.shape[1]

    def emit(val, sl):
        if scale != 1.0:
            val = val * scale
        k = 0
        if emit_f32:
            outs[k][:, sl] = val
            k += 1
        if emit_bf16:
            outs[k][:, sl] = val.astype(BF16)

    if rope:
        cos, sa, sb = cos_ref[...], sa_ref[...], sb_ref[...]
        for gidx in range(tn // LANES):
            sl = slice(gidx * LANES, (gidx + 1) * LANES)
            yg = y[:, sl]
            up = pltpu.roll(yg, LANES - ROPE_DIM // 2, axis=1)
            dn = pltpu.roll(yg, ROPE_DIM // 2, axis=1)
            emit(yg * cos + up * sa + dn * sb, sl)
    else:
        emit(y, slice(None))


def project(h, w, c0, n, *, tm, tn, rope_tabs=None, rope_period=1, scale=1.0,
            emit_f32=True, emit_bf16=False):
    rows, kdim = h.shape
    cb0 = c0 // tn
    rope = rope_tabs is not None
    in_specs = [pl.BlockSpec((tm, kdim), lambda i, j: (i, 0)),
                pl.BlockSpec((kdim, tn), lambda i, j: (0, cb0 + j))]
    args = [h, w]
    if rope:
        tab = pl.BlockSpec((tm, LANES), lambda i, j: (i % rope_period, 0))
        in_specs += [tab, tab, tab]
        args += list(rope_tabs)
    out_shape, out_specs = [], []
    for flag, dt in ((emit_f32, F32), (emit_bf16, BF16)):
        if flag:
            out_shape.append(jax.ShapeDtypeStruct((rows, n), dt))
            out_specs.append(pl.BlockSpec((tm, tn), lambda i, j: (i, j)))
    return pl.pallas_call(
        functools.partial(_proj_kernel, rope=rope, scale=scale, emit_f32=emit_f32, emit_bf16=emit_bf16),
        out_shape=out_shape,
        grid=(rows // tm, n // tn),
        in_specs=in_specs,
        out_specs=out_specs,
        compiler_params=_cparams("parallel", "arbitrary"),
        name="project",
    )(*args)


def rope_tables(pos):
    half = ROPE_DIM // 2
    inv = jnp.float32(ROPE_THETA) ** (-jnp.arange(0, ROPE_DIM, 2, dtype=F32) / ROPE_DIM)
    ang = pos.astype(F32)[:, None] * inv[None, :]
    cos, sin = jnp.cos(ang), jnp.sin(ang)
    r = jnp.arange(LANES) % DA_DIM
    idx = r % half
    lo = (r < half)[None, :]
    hi = ((r >= half) & (r < ROPE_DIM))[None, :]
    cos_t = jnp.where(lo | hi, cos[:, idx], 1.0)
    sa_t = jnp.where(lo, -sin[:, idx], 0.0)
    sb_t = jnp.where(hi, sin[:, idx], 0.0)
    return cos_t, sa_t, sb_t


def _lambda(lq1, lk1, lq2, lk2, lam_init):
    s1 = jnp.sum(lq1 * lk1, axis=-1, keepdims=True)
    s2 = jnp.sum(lq2 * lk2, axis=-1, keepdims=True)
    return jnp.exp(s1) - jnp.exp(s2) + lam_init


def _stack_q(q0, q1):
    q0 = q0.astype(F32)
    q1 = q1.astype(F32)
    lane = lax.broadcasted_iota(jnp.int32, q0.shape, 1)
    first = lane < DA_DIM
    z = jnp.zeros_like(q0)
    return jnp.concatenate([jnp.where(first, q0, z), jnp.where(first, q1, z),
                            jnp.where(first, z, q0), jnp.where(first, z, q1)], axis=0).astype(BF16)


def _lane_tiles(x):
    return [x[:, j * LANES:(j + 1) * LANES] for j in range(x.shape[1] // LANES)]


def _diff_finish(acc, l, lam, t, subln, out_scale):
    o = acc / l
    outs = []
    for g in range(2):
        d = o[g * t:(g + 1) * t] - lam * o[(2 + g) * t:(3 + g) * t]
        outs.append(_rms(d, subln) * out_scale)
    return outs


def _attn_prompt_kernel(q_ref, k_ref, v_ref, lq1, lk1, lq2, lk2, sg_ref, o_ref,
                        qs_scr, s_scr, m_scr, acc_scr, *, tq, kvh, lam_init):
    qi = pl.program_id(1)
    tk = 2 * tq
    n_full = qi // 2
    lam = _lambda(lq1[...], lk1[...], lq2[...], lk2[...], lam_init)
    row = lax.broadcasted_iota(jnp.int32, (4 * tq, tk), 0) % tq + qi * tq
    col = lax.broadcasted_iota(jnp.int32, (4 * tq, tk), 1) + n_full * tk
    visible = col <= row

    def head_lanes(h):
        return slice(h * LANES, (h + 1) * LANES)

    def lane_max(s):
        return functools.reduce(jnp.maximum, _lane_tiles(s))

    def start(h):
        qs_scr[h % 2] = _stack_q(q_ref[0, :, head_lanes(2 * h)], q_ref[0, :, head_lanes(2 * h + 1)])
        m_scr[h % 2] = jnp.full(m_scr.shape[1:], -jnp.inf, F32)

    def scores(h, kp):
        off = pl.multiple_of(kp * tk, tk)
        return lax.dot_general(qs_scr[h % 2], k_ref[0, pl.ds(off, tk), head_lanes(h)], _NT,
                               preferred_element_type=F32)

    def pass1(h, kp):
        s = scores(h, kp)
        s_scr[h % 2, kp] = s
        m_scr[h % 2] = jnp.maximum(m_scr[h % 2], lane_max(s))

    def diag1(h):
        s = jnp.where(visible, scores(h, n_full), -jnp.inf)
        s_scr[h % 2, n_full] = s
        m_row = jnp.max(jnp.maximum(m_scr[h % 2], lane_max(s)), axis=-1, keepdims=True)
        m_scr[h % 2] = jnp.broadcast_to(m_row, m_scr.shape[1:])

    def accumulate(h, s, v):
        m_b = m_scr[h % 2]
        p = jnp.concatenate([jnp.exp2(st - m_b) for st in _lane_tiles(s)], axis=1).astype(BF16)
        v_t = jnp.concatenate([v.astype(F32).T, jnp.ones((BF16_ROWS, v.shape[0]), F32)], axis=0).astype(BF16)
        acc_scr[...] += lax.dot_general(v_t, p, _NT, preferred_element_type=F32)

    def reset2():
        acc_scr[...] = jnp.zeros(acc_scr.shape, F32)

    def pass2(h, kp):
        off = pl.multiple_of(kp * tk, tk)
        accumulate(h, s_scr[h % 2, kp], v_ref[0, pl.ds(off, tk), head_lanes(h)])

    def diag2(h, between=None):
        off = pl.multiple_of(n_full * tk, tk)
        accumulate(h, s_scr[h % 2, n_full, :, :tq], v_ref[0, pl.ds(off, tq), head_lanes(h)])
        if between is not None:
            between()

        @pl.when(qi % 2 == 1)
        def _():
            accumulate(h, s_scr[h % 2, n_full, :, tq:],
                       v_ref[0, pl.ds(pl.multiple_of(off + tq, tq), tq), head_lanes(h)])

        acc = acc_scr[...]
        o_t = acc[:LANES] / acc[LANES:LANES + 1]
        for g in range(2):
            d = o_t[:, g * tq:(g + 1) * tq] - lam * o_t[:, (2 + g) * tq:(3 + g) * tq]
            ms = jnp.mean(d * d, axis=0, keepdims=True)
            y = (d * lax.rsqrt(ms + EPS)) * sg_ref[...] * (1.0 - lam_init)
            o_ref[0, :, head_lanes(2 * h + g)] = y.T.astype(o_ref.dtype)

    def loop(fn):
        def body(kp, carry):
            fn(kp)
            return carry
        lax.fori_loop(0, n_full, body, 0)

    start(0)
    loop(functools.partial(pass1, 0))
    diag1(0)
    for h in range(1, kvh):
        start(h)
        reset2()

        def both(kp, h=h):
            pass1(h, kp)
            pass2(h - 1, kp)

        loop(both)
        diag2(h - 1, between=functools.partial(diag1, h))
    reset2()
    loop(functools.partial(pass2, kvh - 1))
    diag2(kvh - 1)


def attn_prompt(q, k, v, lams, subln, lam_init, *, tq):
    n, L, dq = q.shape
    dk = k.shape[2]
    kvh = dk // LANES
    vec = pl.BlockSpec((1, DA_DIM), lambda b, i: (0, 0))
    subln = subln.reshape(LANES, 1)
    return pl.pallas_call(
        functools.partial(_attn_prompt_kernel, tq=tq, kvh=kvh, lam_init=lam_init),
        out_shape=jax.ShapeDtypeStruct(q.shape, BF16),
        grid=(n, L // tq),
        in_specs=[pl.BlockSpec((1, tq, dq), lambda b, i: (b, i, 0)),
                  pl.BlockSpec((1, L, dk), lambda b, i: (b, 0, 0)),
                  pl.BlockSpec((1, L, dk), lambda b, i: (b, 0, 0)),
                  vec, vec, vec, vec,
                  pl.BlockSpec((LANES, 1), lambda b, i: (0, 0))],
        out_specs=pl.BlockSpec((1, tq, dq), lambda b, i: (b, i, 0)),
        scratch_shapes=[pltpu.VMEM((2, 4 * tq, LANES), BF16),
                        pltpu.VMEM((2, L // (2 * tq), 4 * tq, 2 * tq), F32),
                        pltpu.VMEM((2, 4 * tq, LANES), F32),
                        pltpu.VMEM((LANES + BF16_ROWS, 4 * tq), F32)],
        compiler_params=_cparams("parallel", "arbitrary"),
        name="attn_prompt",
    )(q, k, v, *lams, subln)


def _attn_sample_kernel(pt_ref, q_ref, kn_ref, vn_ref, *rest, n_pages, t, kvh, lam_init):
    kp = rest[:n_pages]
    vp = rest[n_pages:2 * n_pages]
    lq1, lk1, lq2, lk2, sg_ref, o_ref, qs_scr, o_scr = rest[2 * n_pages:]
    r = 4 * t
    page = kp[0].shape[2] // kvh
    for h in range(kvh):
        q0 = q_ref[0, :, (2 * h) * LANES:(2 * h + 1) * LANES]
        q1 = q_ref[0, :, (2 * h + 1) * LANES:(2 * h + 2) * LANES]
        qs_scr[h * r:(h + 1) * r, :] = _stack_q(q0, q1)
    lam = _lambda(lq1[...], lk1[...], lq2[...], lk2[...], lam_init)
    row = lax.broadcasted_iota(jnp.int32, (r, LANES), 0) % t
    col = lax.broadcasted_iota(jnp.int32, (r, LANES), 1)
    new_visible = col <= row

    def head(h, carry):
        def rows_of(ref, lead, n_tok):
            return ref[lead + (pl.ds(h, n_tok, stride=kvh), slice(None))]

        qs = qs_scr[pl.ds(pl.multiple_of(h * r, r), r), :]
        kn = _pad_rows(rows_of(kn_ref, (0,), t), LANES).astype(BF16)
        vn = _pad_rows(rows_of(vn_ref, (0,), t), LANES).astype(BF16)
        tiles = [jnp.dot(qs, rows_of(kp[pg], (0, 0), page).T.astype(BF16),
                         preferred_element_type=F32) for pg in range(n_pages)]
        tiles.append(jnp.where(new_visible, lax.dot_general(qs, kn, _NT, preferred_element_type=F32),
                               -jnp.inf))
        m = jnp.max(functools.reduce(jnp.maximum, tiles), axis=-1, keepdims=True)
        ps = [jnp.exp2(st - m) for st in tiles]
        l = jnp.sum(functools.reduce(jnp.add, ps), axis=-1, keepdims=True)
        acc = jnp.dot(ps[n_pages].astype(BF16), vn, preferred_element_type=F32)
        for pg in range(n_pages):
            acc = acc + jnp.dot(ps[pg].astype(BF16), rows_of(vp[pg], (0, 0), page).astype(BF16),
                                preferred_element_type=F32)
        o0, o1 = _diff_finish(acc, l, lam, t, sg_ref[...], 1.0 - lam_init)
        o_scr[h] = jnp.concatenate([o0, o1], axis=1)
        return carry

    lax.fori_loop(0, kvh, head, 0, unroll=8)
    for h in range(kvh):
        o_ref[0, :, 2 * h * LANES:(2 * h + 2) * LANES] = o_scr[h]


def attn_sample(page_table, q, k_new, v_new, cache_k, cache_v, layer, lams, subln, lam_init):
    n, t, _ = q.shape
    kvh = k_new.shape[2]
    n_pages = page_table.shape[1]
    depth, n_pool, page = cache_k.shape[:3]
    k_new, v_new = (a.reshape(n, t * kvh, LANES) for a in (k_new, v_new))
    cache_k, cache_v = (a.reshape(depth, n_pool, page * kvh, LANES) for a in (cache_k, cache_v))

    def page_spec(pg):
        return pl.BlockSpec((1, 1, page * kvh, LANES), lambda b, pt: (layer, pt[b, pg], 0, 0))

    new_spec = pl.BlockSpec((1, t * kvh, LANES), lambda b, pt: (b, 0, 0))
    q_spec = pl.BlockSpec((1, t, q.shape[2]), lambda b, pt: (b, 0, 0))
    vec = pl.BlockSpec((1, DA_DIM), lambda b, pt: (0, 0))
    pages = [page_spec(pg) for pg in range(n_pages)]
    grid_spec = pltpu.PrefetchScalarGridSpec(
        num_scalar_prefetch=1,
        grid=(n,),
        in_specs=[q_spec, new_spec, new_spec] + pages + pages
                 + [vec, vec, vec, vec, pl.BlockSpec((1, LANES), lambda b, pt: (0, 0))],
        out_specs=q_spec,
        scratch_shapes=[pltpu.VMEM((kvh * 4 * t, LANES), BF16), pltpu.VMEM((kvh, t, 2 * LANES), F32)],
    )
    return pl.pallas_call(
        functools.partial(_attn_sample_kernel, n_pages=n_pages, t=t, kvh=kvh, lam_init=lam_init),
        out_shape=jax.ShapeDtypeStruct(q.shape, F32),
        grid_spec=grid_spec,
        compiler_params=_cparams("parallel"),
        name="attn_sample",
    )(page_table, q, k_new, v_new, *([cache_k] * n_pages), *([cache_v] * n_pages), *lams, subln)


def _split3(x):
    hi = x.astype(BF16)
    r1 = x - hi.astype(F32)
    mid = r1.astype(BF16)
    lo = (r1 - mid.astype(F32)).astype(BF16)
    return hi, mid, lo


def _hgrn_chunk(qb, fb, vb, lb, states, group_ones, off_mask, f_scr, k_scr, v_scr, *, sub, valid):
    c_len, w = qb.shape
    hb = w // LANES
    nsub = c_len // sub
    gw = group_ones.shape[0]
    heads = [slice(h * LANES, (h + 1) * LANES) for h in range(hb)]
    f = lb + (1.0 - lb) * jax.nn.sigmoid(fb)
    if valid < c_len:
        f = jnp.where(lax.broadcasted_iota(jnp.int32, (c_len, w), 0) < valid, f, 1.0)
    lf = jnp.log(f)
    kk = 1.0 - f
    qq = _silu(qb)
    vb16 = vb.astype(BF16)
    for h, sl in enumerate(heads):
        f_scr[h] = f[:, sl]
        k_scr[h] = kk[:, sl]
        v_scr[h] = vb[:, sl]

    def row_bcast(ref, r):
        return jnp.concatenate([jnp.broadcast_to(ref[h, pl.ds(r, 1), :], (sub // 2, LANES))
                                for h in range(hb)], axis=1)

    r_i = lax.broadcasted_iota(jnp.int32, (c_len, c_len), 0)
    c_i = lax.broadcasted_iota(jnp.int32, (c_len, c_len), 1)
    tri = (c_i <= r_i).astype(BF16)
    b = functools.reduce(jnp.add, [jnp.dot(tri, part, preferred_element_type=F32) for part in _split3(lf)])
    b_last = b[c_len - 1:c_len, :]
    qe = (qq * jnp.exp(b)).astype(BF16)
    khat = (kk * jnp.exp(b_last - b)).astype(BF16)
    e_last = jnp.exp(b_last)

    if nsub > 1:
        q_parts, k_parts, v_parts = [], [], []
        for i in range(1, nsub):
            lo = i * sub
            b_ref = b[lo - 1:lo, :]
            q_parts.append((qq[lo:lo + sub] * jnp.exp(b[lo:lo + sub] - b_ref)).astype(BF16))
            k_parts.append((kk[:lo] * jnp.exp(b_ref - b[:lo])).astype(BF16))
            v_parts.append(vb16[:lo])
        q_off, k_off, v_off = (jnp.concatenate(p, axis=0) for p in (q_parts, k_parts, v_parts))
        atts = [lax.dot_general(q_off[:, sl], k_off[:, sl], _NT, preferred_element_type=F32) for sl in heads]

    o_inter = [jnp.dot(qe[:, sl], states[h].astype(BF16), preferred_element_type=F32)
               for h, sl in enumerate(heads)]
    upds = [lax.dot_general(khat[:, sl], vb16[:, sl], _TN, preferred_element_type=F32) for sl in heads]

    half = sub // 2
    trow = lax.broadcasted_iota(jnp.int32, (half, w), 0)
    o_diag = []
    for i in range(nsub):
        lo = i * sub
        q_lo, q_hi = qq[lo:lo + half], qq[lo + half:lo + sub]
        g_hi = jnp.where(trow == half - 1, q_hi, 0.0)
        upper = [None] * half
        upper[half - 1] = g_hi * row_bcast(k_scr, lo + sub - 1)
        for s in range(sub - 2, half - 1, -1):
            g_hi = jnp.where(trow == s - half, q_hi, g_hi * row_bcast(f_scr, lo + s + 1))
            upper[s - half] = g_hi * row_bcast(k_scr, lo + s)
        g_lo = jnp.zeros((half, w), F32)
        both = [None] * half
        for s in range(half - 1, -1, -1):
            f_row = row_bcast(f_scr, lo + s + 1)
            k_row = row_bcast(k_scr, lo + s)
            g_hi = g_hi * f_row
            g_lo = jnp.where(trow == s, q_lo, g_lo * f_row)
            both[s] = jnp.concatenate([g_lo * k_row, g_hi * k_row], axis=0)
        p_all = jnp.concatenate(both + upper, axis=0).astype(BF16)
        rsum = jnp.concatenate(
            [jnp.dot(p_all[:, j * gw:(j + 1) * gw], group_ones, preferred_element_type=F32)
             for j in range(w // gw)], axis=1)
        o_lo = jnp.zeros((half, w), F32)
        o_hi = jnp.zeros((half, w), F32)
        for s in range(half):
            v_row = row_bcast(v_scr, lo + s)
            o_lo = o_lo + rsum[s * sub:s * sub + half] * v_row
            o_hi = o_hi + rsum[s * sub + half:(s + 1) * sub] * v_row
        for j in range(half):
            base = half * sub + j * half
            o_hi = o_hi + rsum[base:base + half] * row_bcast(v_scr, lo + half + j)
        o_diag += [o_lo, o_hi]
    o_diag = jnp.concatenate(o_diag, axis=0)

    outs, new_states = [], []
    for h, sl in enumerate(heads):
        o_h = o_inter[h] + o_diag[:, sl]
        if nsub > 1:
            att = jnp.where(off_mask, atts[h], 0.0).astype(BF16)
            o_off = jnp.dot(att, v_off[:, sl], preferred_element_type=F32)
            o_h = o_h + jnp.concatenate([jnp.zeros((sub, LANES), F32), o_off], axis=0)
        outs.append(o_h)
        decay = jnp.broadcast_to(e_last[:, sl], (LANES, LANES)).T
        new_states.append(decay * states[h] + upds[h])
    return outs, new_states


def _hgrn_kernel(q_ref, f_ref, v_ref, lb_ref, g_ref, *rest, hb, chunk, sub, has_init):
    if has_init:
        s0_ref, o_ref, s_ref, f_scr, k_scr, v_scr = rest
    else:
        o_ref, s_ref, f_scr, k_scr, v_scr = rest
    tb = pl.program_id(2)

    @pl.when(tb == 0)
    def _():
        if has_init:
            s_ref[...] = s0_ref[...]
        else:
            s_ref[...] = jnp.zeros(s_ref.shape, F32)

    gw = MXU_COLS if (hb * LANES) % MXU_COLS == 0 else LANES
    gr = lax.broadcasted_iota(jnp.int32, (gw, gw), 0) // LANES
    gc = lax.broadcasted_iota(jnp.int32, (gw, gw), 1) // LANES
    group_ones = (gr == gc).astype(BF16)
    nsub = chunk // sub
    off_mask = None
    if nsub > 1:
        shape = ((nsub - 1) * sub, sub * nsub * (nsub - 1) // 2)
        qblk = lax.broadcasted_iota(jnp.int32, shape, 0) // sub + 1
        kcol = lax.broadcasted_iota(jnp.int32, shape, 1)
        off_mask = functools.reduce(jnp.logical_or, [
            (qblk == i) & (kcol >= sub * i * (i - 1) // 2) & (kcol < sub * i * (i + 1) // 2)
            for i in range(1, nsub)])
    t_blk = q_ref.shape[1]

    def run(rows, valid):
        tile = lambda ref: _pad_rows(ref[0, rows, :], chunk)
        outs, new = _hgrn_chunk(tile(q_ref), tile(f_ref), tile(v_ref), lb_ref[...],
                                [s_ref[0, hh] for hh in range(hb)], group_ones, off_mask,
                                f_scr, k_scr, v_scr, sub=sub, valid=valid)
        for hh in range(hb):
            s_ref[0, hh] = new[hh]
            o_ref[0, rows, hh * LANES:(hh + 1) * LANES] = _rms(outs[hh][:valid], g_ref[...])

    if t_blk < chunk:
        run(slice(None), t_blk)
        return

    def body(ci, carry):
        run(pl.ds(pl.multiple_of(ci * chunk, chunk), chunk), chunk)
        return carry

    lax.fori_loop(0, t_blk // chunk, body, 0, unroll=4)


def hgrn(rest, lb, hg_g, s0, layer, *, n, L, tb, hb, chunk, sub):
    d = HG_HEADS * HG_DK
    x3 = rest.reshape(n, L, rest.shape[1])
    nhb = HG_HEADS // hb
    w = hb * LANES
    has_init = s0 is not None

    def col_spec(seg):
        return pl.BlockSpec((1, tb, w), lambda b, h, t: (b, t, seg * (d // w) + h))

    in_specs = [col_spec(0), col_spec(1), col_spec(2),
                pl.BlockSpec((1, w), lambda b, h, t: (0, h)),
                pl.BlockSpec((1, LANES), lambda b, h, t: (0, 0))]
    args = [x3, x3, x3, lb, hg_g]
    st_spec = pl.BlockSpec((1, hb, HG_DK, LANES), lambda b, h, t: (b, h, 0, 0))
    if has_init:
        in_specs.append(pl.BlockSpec((1, 1, hb, HG_DK, LANES), lambda b, h, t: (layer, b, h, 0, 0)))
        args.append(s0)
    kernel_fn = functools.partial(_hgrn_kernel, hb=hb, chunk=chunk, sub=sub, has_init=has_init)
    if has_init:
        inner = kernel_fn

        def kernel_fn(q_ref, f_ref, v_ref, lb_ref, g_ref, s0_ref, *rest):
            inner(q_ref, f_ref, v_ref, lb_ref, g_ref, s0_ref.at[0], *rest)

    return pl.pallas_call(
        kernel_fn,
        out_shape=[jax.ShapeDtypeStruct((n, L, d), F32),
                   jax.ShapeDtypeStruct((n, HG_HEADS, HG_DK, LANES), F32)],
        grid=(n, nhb, L // tb),
        in_specs=in_specs,
        out_specs=[pl.BlockSpec((1, tb, w), lambda b, h, t: (b, t, h)), st_spec],
        scratch_shapes=[pltpu.VMEM((hb, chunk, LANES), F32)] * 3,
        compiler_params=_cparams("parallel", "parallel", "arbitrary"),
        name="hgrn",
    )(*args)


def _out_tail(a, w_ref, x_ref, post_ref, nxt_ref, y_ref, hn_ref):
    t = jnp.dot(a, w_ref[...], preferred_element_type=F32)
    y = x_ref[...] + _rms(t, post_ref[...])
    y_ref[...] = y
    hn_ref[...] = _rms(y, nxt_ref[...]).astype(BF16)


def _outproj_kernel(a_ref, w_ref, x_ref, post_ref, nxt_ref, y_ref, hn_ref):
    _out_tail(a_ref[...].astype(BF16), w_ref, x_ref, post_ref, nxt_ref, y_ref, hn_ref)


def _merge_kernel(ya_ref, yb_ref, go_ref, ga_ref, gb_ref, w_ref, x_ref, post_ref, nxt_ref, y_ref, hn_ref):
    m = (jax.nn.sigmoid(ga_ref[...]) * ya_ref[...].astype(F32)
         + jax.nn.sigmoid(gb_ref[...]) * (yb_ref[...] * _silu(go_ref[...])))
    _out_tail(m.astype(BF16), w_ref, x_ref, post_ref, nxt_ref, y_ref, hn_ref)


def _out_common(kernel, lead_args, lead_specs, w, x, post, nxt, tm, name):
    rows, d = x.shape
    kdim = w.shape[0]
    vec = pl.BlockSpec((1, d), lambda i: (0, 0))
    row = pl.BlockSpec((tm, d), lambda i: (i, 0))
    return pl.pallas_call(
        kernel,
        out_shape=[jax.ShapeDtypeStruct((rows, d), F32), jax.ShapeDtypeStruct((rows, d), BF16)],
        grid=(rows // tm,),
        in_specs=lead_specs + [pl.BlockSpec((kdim, d), lambda i: (0, 0)), row, vec, vec],
        out_specs=[row, row],
        compiler_params=_cparams("parallel"),
        name=name,
    )(*lead_args, w, x, post, nxt)


def outproj(a, w, x, post, nxt, *, tm):
    spec = pl.BlockSpec((tm, a.shape[1]), lambda i: (i, 0))
    return _out_common(_outproj_kernel, [a], [spec], w, x, post, nxt, tm, "outproj")


def merge(ya, yb, rest, w, x, post, nxt, *, tm):
    d = x.shape[1]
    row = pl.BlockSpec((tm, d), lambda i: (i, 0))
    gate = lambda seg: pl.BlockSpec((tm, d), lambda i: (i, seg))
    return _out_common(_merge_kernel, [ya, yb, rest, rest, rest], [row, row, gate(3), gate(4), gate(5)],
                       w, x, post, nxt, tm, "merge")


def _cross_kernel(q_ref, k_ref, v_ref, o_ref, *, nb, heads, native):
    scale = CA_DIM ** -0.5 * LOG2E
    for b in range(nb):
        for h in range(heads):
            sl = slice(h * CA_DIM, (h + 1) * CA_DIM)
            q = q_ref[b, :, sl]
            tq = q.shape[0]
            if tq % BF16_ROWS:
                q = _pad_rows(q.astype(F32), BF16_ROWS * pl.cdiv(tq, BF16_ROWS))
            if native:
                k, v = k_ref[0, b, :, h, :], v_ref[0, b, :, h, :]
            else:
                k, v = k_ref[b, :, sl], v_ref[b, :, sl]
            s = lax.dot_general(q.astype(BF16), k.astype(BF16), _NT, preferred_element_type=F32) * scale
            p = jnp.exp2(s - jnp.max(s, axis=-1, keepdims=True))
            l = jnp.sum(p, axis=-1, keepdims=True)
            o = jnp.dot(p.astype(BF16), v.astype(BF16), preferred_element_type=F32) / l
            o_ref[b, :, sl] = o[:tq].astype(o_ref.dtype)


def cross_attend(q, mk, mv, layer, *, nb, tq):
    n, L, w = q.shape
    native = mk.ndim == 5
    qspec = pl.BlockSpec((nb, tq, w), lambda b, i: (b, i, 0))
    if native:
        kspec = pl.BlockSpec((1, nb) + mk.shape[2:], lambda b, i: (layer, b, 0, 0, 0))
    else:
        kspec = pl.BlockSpec((nb,) + mk.shape[1:], lambda b, i: (b, 0, 0))
    return pl.pallas_call(
        functools.partial(_cross_kernel, nb=nb, heads=w // CA_DIM, native=native),
        out_shape=jax.ShapeDtypeStruct(q.shape, q.dtype),
        grid=(n // nb, L // tq),
        in_specs=[qspec, kspec, kspec],
        out_specs=qspec,
        compiler_params=_cparams("parallel", "arbitrary"),
        name="cross_attend",
    )(q, mk, mv)


def _row(v):
    return v.reshape(1, -1).astype(F32)


def _group_step(x, pos, n, L, layer, lw, lower, attn_fn, s0, mem_k, mem_v, cfg):
    d = x.shape[1]
    tm = cfg["tm"]
    x1, h1 = ffn_half(x, lw["ffn1_pre"], lw["ffn1_post"], lw["mix_pre"], lw["ffn1_w_gu"], lw["ffn1_w_down"],
                      tm=cfg["tm_ffn"], tf=cfg["tf"], emit_next=True)

    tabs = rope_tables(pos)
    period = max(L // tm, 1)
    if L < tm:
        tabs = tuple(jnp.tile(t, (tm // L, 1)) for t in tabs)
    nq = DA_HEADS * 2 * DA_DIM
    nk = DA_KV_HEADS * 2 * DA_DIM
    w_in = lw["w_in"]
    short = L % BF16_ROWS != 0
    (qa,) = project(h1, w_in, 0, nq, tm=tm, tn=cfg["tn"], rope_tabs=tabs, rope_period=period,
                    scale=DA_DIM ** -0.5 * LOG2E, emit_f32=short, emit_bf16=not short)
    kv_out = project(h1, w_in, nq, nk, tm=tm, tn=cfg["tn"], rope_tabs=tabs, rope_period=period,
                     emit_f32=True, emit_bf16=not short)
    vv_out = project(h1, w_in, nq + nk, nk, tm=tm, tn=cfg["tn"], emit_f32=True, emit_bf16=not short)
    k_rows = kv_out[0].reshape(n, L, DA_KV_HEADS, 2 * DA_DIM)
    v_rows = vv_out[0].reshape(n, L, DA_KV_HEADS, 2 * DA_DIM)
    (rest,) = project(h1, w_in, nq + 2 * nk, 6 * d, tm=tm, tn=cfg["tn"], emit_f32=True, emit_bf16=False)

    if short:
        ya = attn_fn(qa.reshape(n, L, nq), k_rows, v_rows)
    else:
        ya = attn_fn(qa.reshape(n, L, nq), kv_out[1].reshape(n, L, nk), vv_out[1].reshape(n, L, nk))
    yb, s_new = hgrn(rest, lower, lw["hg_norm_g"], s0, layer, n=n, L=L, tb=cfg["hg_tb"], hb=cfg["hg_hb"],
                     chunk=cfg["hg_chunk"], sub=cfg["hg_sub"])
    x2, h2 = merge(ya.reshape(n * L, d), yb.reshape(n * L, d), rest, lw["w_out"], x1,
                   lw["mix_post"], lw["ca_pre"], tm=cfg["tm_merge"])

    (qc,) = project(h2, lw["w_cq"], 0, CA_HEADS * CA_DIM, tm=tm, tn=CA_HEADS * CA_DIM,
                    emit_f32=short, emit_bf16=not short)
    oc = cross_attend(qc.reshape(n, L, -1), mem_k, mem_v, layer, nb=cfg["ca_nb"], tq=cfg["ca_tq"])
    x3, _ = outproj(oc.reshape(n * L, -1), lw["w_co"], x2, lw["ca_post"], lw["ffn2_pre"], tm=cfg["tm_merge"])

    x4, _ = ffn_half(x3, lw["ffn2_pre"], lw["ffn2_post"], lw["ffn2_pre"], lw["ffn2_w_gu"], lw["ffn2_w_down"],
                     tm=cfg["tm_ffn"], tf=cfg["tf"], emit_next=False)
    return x4, k_rows, v_rows, s_new


def kernel(x_prompt, x_sample, cache_k, cache_v, state_hgrn, cache_mem_k, cache_mem_v, page_table, mem_prompt, ffn1_pre, ffn1_post, ffn1_w_gu, ffn1_w_down, mix_pre, mix_post, w_in, w_out, lambda_q1, lambda_k1, lambda_q2, lambda_k2, subln_g, hg_norm_g, hg_lb_logits, ca_pre, ca_post, mem_norm_g, w_cq, w_ckv, w_co, ffn2_pre, ffn2_post, ffn2_w_gu, ffn2_w_down):
    n_p, seq, d = x_prompt.shape
    n_s, dec_seq, _ = x_sample.shape
    depth = ffn1_pre.shape[0]
    page = cache_k.shape[2]
    past = page_table.shape[1] * page
    mem_len = mem_prompt.shape[1]
    ca_w = CA_HEADS * CA_DIM

    lower = jnp.cumsum(jax.nn.softmax(hg_lb_logits.astype(F32), axis=0), axis=0)
    xp = x_prompt.reshape(n_p * seq, d)
    xs = x_sample.reshape(n_s * dec_seq, d)
    pos_p = jnp.arange(seq)
    pos_s = past + jnp.arange(dec_seq)

    rows_p, rows_s = n_p * seq, n_s * dec_seq
    cfg_p = dict(tm=min(1024, rows_p), tn=1024, tm_ffn=min(512, rows_p), tf=512, tm_merge=min(256, rows_p),
                 hg_tb=min(512, seq), hg_hb=4, hg_chunk=64, hg_sub=16, ca_nb=1, ca_tq=min(512, seq))
    cfg_s = dict(tm=min(1024, rows_s), tn=1024, tm_ffn=min(512, rows_s), tf=512, tm_merge=min(256, rows_s),
                 hg_tb=dec_seq, hg_hb=16, hg_chunk=BF16_ROWS, hg_sub=BF16_ROWS, ca_nb=min(4, n_s), ca_tq=dec_seq)

    outs = [[] for _ in range(8)]
    for l in range(depth):
        lam_init = 0.8 - 0.6 * math.exp(-0.3 * l)
        lw = {
            "ffn1_pre": _row(ffn1_pre[l]), "ffn1_post": _row(ffn1_post[l]),
            "ffn1_w_gu": ffn1_w_gu[l].astype(BF16), "ffn1_w_down": ffn1_w_down[l].astype(BF16),
            "mix_pre": _row(mix_pre[l]), "mix_post": _row(mix_post[l]),
            "w_in": w_in[l].astype(BF16), "w_out": w_out[l].astype(BF16),
            "hg_norm_g": _row(hg_norm_g[l]),
            "ca_pre": _row(ca_pre[l]), "ca_post": _row(ca_post[l]),
            "w_cq": w_cq[l].astype(BF16), "w_co": w_co[l].astype(BF16),
            "ffn2_pre": _row(ffn2_pre[l]), "ffn2_post": _row(ffn2_post[l]),
            "ffn2_w_gu": ffn2_w_gu[l].astype(BF16), "ffn2_w_down": ffn2_w_down[l].astype(BF16),
        }
        lams = [_row(lambda_q1[l]), _row(lambda_k1[l]), _row(lambda_q2[l]), _row(lambda_k2[l])]
        subln = _row(subln_g[l])
        lb = _row(lower[l])

        rows_m = n_p * mem_len
        hm = rmsnorm_bf16(mem_prompt.reshape(rows_m, d), _row(mem_norm_g[l]), tm=min(512, rows_m))
        w_ckv_l = w_ckv[l].astype(BF16)
        tm_m = min(1024, rows_m)
        mk32, mk16 = project(hm, w_ckv_l, 0, ca_w, tm=tm_m, tn=ca_w, emit_f32=True, emit_bf16=True)
        mv32, mv16 = project(hm, w_ckv_l, ca_w, ca_w, tm=tm_m, tn=ca_w, emit_f32=True, emit_bf16=True)

        attn_p = functools.partial(attn_prompt, lams=lams, subln=subln, lam_init=lam_init,
                                   tq=min(256, seq))
        xp, kp, vp, sp = _group_step(xp, pos_p, n_p, seq, l, lw, lb, attn_p, None,
                                     mk16.reshape(n_p, mem_len, ca_w), mv16.reshape(n_p, mem_len, ca_w), cfg_p)

        def attn_s(q, k, v):
            return attn_sample(page_table, q, k, v, cache_k, cache_v, l, lams, subln, lam_init)

        xs, ks, vs, ss = _group_step(xs, pos_s, n_s, dec_seq, l, lw, lb, attn_s, state_hgrn,
                                     cache_mem_k, cache_mem_v, cfg_s)

        for lst, val in zip(outs, (
                kp, vp, sp, mk32.reshape(n_p, mem_len, CA_HEADS, CA_DIM),
                mv32.reshape(n_p, mem_len, CA_HEADS, CA_DIM), ks, vs, ss)):
            lst.append(val)

    return (xp.reshape(n_p, seq, d), xs.reshape(n_s, dec_seq, d), *[jnp.stack(o) for o in outs])
```

```python
import functools
import math

import jax
import jax.numpy as jnp
from jax import lax
from jax.experimental import pallas as pl
from jax.experimental.pallas import tpu as pltpu

F32 = jnp.float32
BF16 = jnp.bfloat16

EPS = 1e-6
ROPE_THETA = 500000.0
LANES = 128
BF16_ROWS = 16
MXU_COLS = 256
VMEM_LIMIT_BYTES = 56 * 1024 * 1024
LOG2E = math.log2(math.e)

DA_HEADS = 16
DA_KV_HEADS = 8
DA_DIM = 64
ROPE_DIM = DA_DIM // 4
HG_HEADS = 16
HG_DK = 128
CA_HEADS = 4
CA_DIM = 128


def _cparams(*sem):
    return pltpu.CompilerParams(dimension_semantics=sem, vmem_limit_bytes=VMEM_LIMIT_BYTES)


def _rms(x, g):
    ms = jnp.mean(x * x, axis=-1, keepdims=True)
    return (x * lax.rsqrt(ms + EPS)) * g


def _silu(x):
    return x * jax.nn.sigmoid(x)


def _pad_rows(x, rows):
    if x.shape[0] == rows:
        return x
    return jnp.concatenate([x, jnp.zeros((rows - x.shape[0], x.shape[1]), x.dtype)], axis=0)


_NT = (((1,), (1,)), ((), ()))
_TN = (((0,), (0,)), ((), ()))


def _rmsnorm_kernel(x_ref, g_ref, o_ref):
    o_ref[...] = _rms(x_ref[...], g_ref[...]).astype(o_ref.dtype)


def rmsnorm_bf16(x, g, *, tm):
    rows, d = x.shape
    return pl.pallas_call(
        _rmsnorm_kernel,
        out_shape=jax.ShapeDtypeStruct((rows, d), BF16),
        grid=(rows // tm,),
        in_specs=[pl.BlockSpec((tm, d), lambda i: (i, 0)),
                  pl.BlockSpec((1, d), lambda i: (0, 0))],
        out_specs=pl.BlockSpec((tm, d), lambda i: (i, 0)),
        compiler_params=_cparams("parallel"),
        name="rmsnorm",
    )(x, g)


def _ffn_kernel(x_ref, pre_ref, post_ref, nxt_ref, wg_ref, wu_ref, wd_ref, *rest, emit_next):
    if emit_next:
        y_ref, hn_ref, h_scr, acc_scr = rest
    else:
        y_ref, h_scr, acc_scr = rest
    j = pl.program_id(1)

    @pl.when(j == 0)
    def _():
        h_scr[...] = _rms(x_ref[...], pre_ref[...]).astype(BF16)
        acc_scr[...] = jnp.zeros(acc_scr.shape, F32)

    h = h_scr[...]
    g = jnp.dot(h, wg_ref[...], preferred_element_type=F32)
    u = jnp.dot(h, wu_ref[...], preferred_element_type=F32)
    a = (_silu(g) * u).astype(BF16)
    cw = 2 * MXU_COLS
    for c in range(acc_scr.shape[1] // cw):
        sl = slice(c * cw, (c + 1) * cw)
        acc_scr[:, sl] += jnp.dot(a, wd_ref[:, sl], preferred_element_type=F32)

    @pl.when(j == pl.num_programs(1) - 1)
    def _():
        y = x_ref[...] + 0.5 * _rms(acc_scr[...], post_ref[...])
        y_ref[...] = y
        if emit_next:
            hn_ref[...] = _rms(y, nxt_ref[...]).astype(BF16)


def ffn_half(x, pre, post, nxt, w_gu, w_down, *, tm, tf, emit_next):
    rows, d = x.shape
    d_ff = w_down.shape[0]
    nj = d_ff // tf
    out_shape = [jax.ShapeDtypeStruct((rows, d), F32)]
    out_specs = [pl.BlockSpec((tm, d), lambda i, j: (i, 0))]
    if emit_next:
        out_shape.append(jax.ShapeDtypeStruct((rows, d), BF16))
        out_specs.append(pl.BlockSpec((tm, d), lambda i, j: (i, 0)))
    vec = pl.BlockSpec((1, d), lambda i, j: (0, 0))
    res = pl.pallas_call(
        functools.partial(_ffn_kernel, emit_next=emit_next),
        out_shape=out_shape,
        grid=(rows // tm, nj),
        in_specs=[pl.BlockSpec((tm, d), lambda i, j: (i, 0)), vec, vec, vec,
                  pl.BlockSpec((d, tf), lambda i, j: (0, j)),
                  pl.BlockSpec((d, tf), lambda i, j: (0, j + nj)),
                  pl.BlockSpec((tf, d), lambda i, j: (j, 0))],
        out_specs=out_specs,
        scratch_shapes=[pltpu.VMEM((tm, d), BF16), pltpu.VMEM((tm, d), F32)],
        compiler_params=_cparams("parallel", "arbitrary"),
        name="ffn_half",
    )(x, pre, post, nxt, w_gu, w_gu, w_down)
    return res if emit_next else (res[0], None)


def _proj_kernel(h_ref, w_ref, *rest, rope, scale, emit_f32, emit_bf16):
    if rope:
        cos_ref, sa_ref, sb_ref = rest[:3]
        outs = rest[3:]
    else:
        outs = rest
    y = jnp.dot(h_ref[...], w_ref[...], preferred_element_type=F32)
    tn = y.shape[1]

    def emit(val, sl):
        if scale != 1.0:
            val = val * scale
        k = 0
        if emit_f32:
            outs[k][:, sl] = val
            k += 1
        if emit_bf16:
            outs[k][:, sl] = val.astype(BF16)

    if rope:
        cos, sa, sb = cos_ref[...], sa_ref[...], sb_ref[...]
        for gidx in range(tn // LANES):
            sl = slice(gidx * LANES, (gidx + 1) * LANES)
            yg = y[:, sl]
            up = pltpu.roll(yg, LANES - ROPE_DIM // 2, axis=1)
            dn = pltpu.roll(yg, ROPE_DIM // 2, axis=1)
            emit(yg * cos + up * sa + dn * sb, sl)
    else:
        emit(y, slice(None))


def project(h, w, c0, n, *, tm, tn, rope_tabs=None, rope_period=1, scale=1.0,
            emit_f32=True, emit_bf16=False):
    rows, kdim = h.shape
    cb0 = c0 // tn
    rope = rope_tabs is not None
    in_specs = [pl.BlockSpec((tm, kdim), lambda i, j: (i, 0)),
                pl.BlockSpec((kdim, tn), lambda i, j: (0, cb0 + j))]
    args = [h, w]
    if rope:
        tab = pl.BlockSpec((tm, LANES), lambda i, j: (i % rope_period, 0))
        in_specs += [tab, tab, tab]
        args += list(rope_tabs)
    out_shape, out_specs = [], []
    for flag, dt in ((emit_f32, F32), (emit_bf16, BF16)):
        if flag:
            out_shape.append(jax.ShapeDtypeStruct((rows, n), dt))
            out_specs.append(pl.BlockSpec((tm, tn), lambda i, j: (i, j)))
    return pl.pallas_call(
        functools.partial(_proj_kernel, rope=rope, scale=scale, emit_f32=emit_f32, emit_bf16=emit_bf16),
        out_shape=out_shape,
        grid=(rows // tm, n // tn),
        in_specs=in_specs,
        out_specs=out_specs,
        compiler_params=_cparams("parallel", "arbitrary"),
        name="project",
    )(*args)


def rope_tables(pos):
    half = ROPE_DIM // 2
    inv = jnp.float32(ROPE_THETA) ** (-jnp.arange(0, ROPE_DIM, 2, dtype=F32) / ROPE_DIM)
    ang = pos.astype(F32)[:, None] * inv[None, :]
    cos, sin = jnp.cos(ang), jnp.sin(ang)
    r = jnp.arange(LANES) % DA_DIM
    idx = r % half
    lo = (r < half)[None, :]
    hi = ((r >= half) & (r < ROPE_DIM))[None, :]
    cos_t = jnp.where(lo | hi, cos[:, idx], 1.0)
    sa_t = jnp.where(lo, -sin[:, idx], 0.0)
    sb_t = jnp.where(hi, sin[:, idx], 0.0)
    return cos_t, sa_t, sb_t


def _lambda(lq1, lk1, lq2, lk2, lam_init):
    s1 = jnp.sum(lq1 * lk1, axis=-1, keepdims=True)
    s2 = jnp.sum(lq2 * lk2, axis=-1, keepdims=True)
    return jnp.exp(s1) - jnp.exp(s2) + lam_init


def _stack_q(q0, q1):
    q0 = q0.astype(F32)
    q1 = q1.astype(F32)
    lane = lax.broadcasted_iota(jnp.int32, q0.shape, 1)
    first = lane < DA_DIM
    z = jnp.zeros_like(q0)
    return jnp.concatenate([jnp.where(first, q0, z), jnp.where(first, q1, z),
                            jnp.where(first, z, q0), jnp.where(first, z, q1)], axis=0).astype(BF16)


def _lane_tiles(x):
    return [x[:, j * LANES:(j + 1) * LANES] for j in range(x.shape[1] // LANES)]


def _diff_finish(acc, l, lam, t, subln, out_scale):
    o = acc / l
    outs = []
    for g in range(2):
        d = o[g * t:(g + 1) * t] - lam * o[(2 + g) * t:(3 + g) * t]
        outs.append(_rms(d, subln) * out_scale)
    return outs


def _attn_prompt_kernel(q_ref, k_ref, v_ref, lq1, lk1, lq2, lk2, sg_ref, o_ref,
                        qs_scr, s_scr, m_scr, acc_scr, *, tq, kvh, lam_init):
    qi = pl.program_id(1)
    tk = 2 * tq
    n_full = qi // 2
    lam = _lambda(lq1[...], lk1[...], lq2[...], lk2[...], lam_init)
    row = lax.broadcasted_iota(jnp.int32, (4 * tq, tk), 0) % tq + qi * tq
    col = lax.broadcasted_iota(jnp.int32, (4 * tq, tk), 1) + n_full * tk
    visible = col <= row

    def head_lanes(h):
        return slice(h * LANES, (h + 1) * LANES)

    def lane_max(s):
        return functools.reduce(jnp.maximum, _lane_tiles(s))

    def start(h):
        qs_scr[h % 2] = _stack_q(q_ref[0, :, head_lanes(2 * h)], q_ref[0, :, head_lanes(2 * h + 1)])
        m_scr[h % 2] = jnp.full(m_scr.shape[1:], -jnp.inf, F32)

    def scores(h, kp):
        off = pl.multiple_of(kp * tk, tk)
        return lax.dot_general(qs_scr[h % 2], k_ref[0, pl.ds(off, tk), head_lanes(h)], _NT,
                               preferred_element_type=F32)

    def pass1(h, kp):
        s = scores(h, kp)
        s_scr[h % 2, kp] = s
        m_scr[h % 2] = jnp.maximum(m_scr[h % 2], lane_max(s))

    def diag1(h):
        s = jnp.where(visible, scores(h, n_full), -jnp.inf)
        s_scr[h % 2, n_full] = s
        m_row = jnp.max(jnp.maximum(m_scr[h % 2], lane_max(s)), axis=-1, keepdims=True)
        m_scr[h % 2] = jnp.broadcast_to(m_row, m_scr.shape[1:])

    def accumulate(h, s, v):
        m_b = m_scr[h % 2]
        p = jnp.concatenate([jnp.exp2(st - m_b) for st in _lane_tiles(s)], axis=1).astype(BF16)
        v_t = jnp.concatenate([v.astype(F32).T, jnp.ones((BF16_ROWS, v.shape[0]), F32)], axis=0).astype(BF16)
        acc_scr[...] += lax.dot_general(v_t, p, _NT, preferred_element_type=F32)

    def reset2():
        acc_scr[...] = jnp.zeros(acc_scr.shape, F32)

    def pass2(h, kp):
        off = pl.multiple_of(kp * tk, tk)
        accumulate(h, s_scr[h % 2, kp], v_ref[0, pl.ds(off, tk), head_lanes(h)])

    def diag2(h, between=None):
        off = pl.multiple_of(n_full * tk, tk)
        accumulate(h, s_scr[h % 2, n_full], v_ref[0, pl.ds(off, tk), head_lanes(h)])
        if between is not None:
            between()

        acc = acc_scr[...]
        o_t = acc[:LANES] / acc[LANES:LANES + 1]
        for g in range(2):
            d = o_t[:, g * tq:(g + 1) * tq] - lam * o_t[:, (2 + g) * tq:(3 + g) * tq]
            ms = jnp.mean(d * d, axis=0, keepdims=True)
            y = (d * lax.rsqrt(ms + EPS)) * sg_ref[...] * (1.0 - lam_init)
            o_ref[0, :, head_lanes(2 * h + g)] = y.T.astype(o_ref.dtype)

    def loop(fn):
        def body(kp, carry):
            fn(kp)
            return carry
        lax.fori_loop(0, n_full, body, 0)

    start(0)
    loop(functools.partial(pass1, 0))
    diag1(0)
    for h in range(1, kvh):
        start(h)
        reset2()

        def both(kp, h=h):
            pass1(h, kp)
            pass2(h - 1, kp)

        loop(both)
        diag2(h - 1, between=functools.partial(diag1, h))
    reset2()
    loop(functools.partial(pass2, kvh - 1))
    diag2(kvh - 1)


def attn_prompt(q, k, v, lams, subln, lam_init, *, tq):
    n, L, dq = q.shape
    dk = k.shape[2]
    kvh = dk // LANES
    vec = pl.BlockSpec((1, DA_DIM), lambda b, i: (0, 0))
    subln = subln.reshape(LANES, 1)
    return pl.pallas_call(
        functools.partial(_attn_prompt_kernel, tq=tq, kvh=kvh, lam_init=lam_init),
        out_shape=jax.ShapeDtypeStruct(q.shape, BF16),
        grid=(n, L // tq),
        in_specs=[pl.BlockSpec((1, tq, dq), lambda b, i: (b, i, 0)),
                  pl.BlockSpec((1, L, dk), lambda b, i: (b, 0, 0)),
                  pl.BlockSpec((1, L, dk), lambda b, i: (b, 0, 0)),
                  vec, vec, vec, vec,
                  pl.BlockSpec((LANES, 1), lambda b, i: (0, 0))],
        out_specs=pl.BlockSpec((1, tq, dq), lambda b, i: (b, i, 0)),
        scratch_shapes=[pltpu.VMEM((2, 4 * tq, LANES), BF16),
                        pltpu.VMEM((2, L // (2 * tq), 4 * tq, 2 * tq), F32),
                        pltpu.VMEM((2, 4 * tq, LANES), F32),
                        pltpu.VMEM((LANES + BF16_ROWS, 4 * tq), F32)],
        compiler_params=_cparams("parallel", "arbitrary"),
        name="attn_prompt",
    )(q, k, v, *lams, subln)


def _attn_sample_kernel(pt_ref, q_ref, kn_ref, vn_ref, *rest, n_pages, t, kvh, lam_init):
    kp = rest[:n_pages]
    vp = rest[n_pages:2 * n_pages]
    lq1, lk1, lq2, lk2, sg_ref, o_ref, qs_scr, o_scr = rest[2 * n_pages:]
    r = 4 * t
    page = kp[0].shape[2] // kvh
    for h in range(kvh):
        q0 = q_ref[0, :, (2 * h) * LANES:(2 * h + 1) * LANES]
        q1 = q_ref[0, :, (2 * h + 1) * LANES:(2 * h + 2) * LANES]
        qs_scr[h * r:(h + 1) * r, :] = _stack_q(q0, q1)
    lam = _lambda(lq1[...], lk1[...], lq2[...], lk2[...], lam_init)
    row = lax.broadcasted_iota(jnp.int32, (r, LANES), 0) % t
    col = lax.broadcasted_iota(jnp.int32, (r, LANES), 1)
    new_visible = col <= row

    def head(h, carry):
        def rows_of(ref, lead, n_tok):
            return ref[lead + (pl.ds(h, n_tok, stride=kvh), slice(None))]

        qs = qs_scr[pl.ds(pl.multiple_of(h * r, r), r), :]
        kn = _pad_rows(rows_of(kn_ref, (0,), t), LANES).astype(BF16)
        vn = _pad_rows(rows_of(vn_ref, (0,), t), LANES).astype(BF16)
        tiles = [jnp.dot(qs, rows_of(kp[pg], (0, 0), page).T.astype(BF16),
                         preferred_element_type=F32) for pg in range(n_pages)]
        tiles.append(jnp.where(new_visible, lax.dot_general(qs, kn, _NT, preferred_element_type=F32),
                               -jnp.inf))
        m = jnp.max(functools.reduce(jnp.maximum, tiles), axis=-1, keepdims=True)
        ps = [jnp.exp2(st - m) for st in tiles]
        l = jnp.sum(functools.reduce(jnp.add, ps), axis=-1, keepdims=True)
        acc = jnp.dot(ps[n_pages].astype(BF16), vn, preferred_element_type=F32)
        for pg in range(n_pages):
            acc = acc + jnp.dot(ps[pg].astype(BF16), rows_of(vp[pg], (0, 0), page).astype(BF16),
                                preferred_element_type=F32)
        o0, o1 = _diff_finish(acc, l, lam, t, sg_ref[...], 1.0 - lam_init)
        o_scr[h] = jnp.concatenate([o0, o1], axis=1)
        return carry

    lax.fori_loop(0, kvh, head, 0, unroll=8)
    for h in range(kvh):
        o_ref[0, :, 2 * h * LANES:(2 * h + 2) * LANES] = o_scr[h]


def attn_sample(page_table, q, k_new, v_new, cache_k, cache_v, layer, lams, subln, lam_init):
    n, t, _ = q.shape
    kvh = k_new.shape[2]
    n_pages = page_table.shape[1]
    depth, n_pool, page = cache_k.shape[:3]
    k_new, v_new = (a.reshape(n, t * kvh, LANES) for a in (k_new, v_new))
    cache_k, cache_v = (a.reshape(depth, n_pool, page * kvh, LANES) for a in (cache_k, cache_v))

    def page_spec(pg):
        return pl.BlockSpec((1, 1, page * kvh, LANES), lambda b, pt: (layer, pt[b, pg], 0, 0))

    new_spec = pl.BlockSpec((1, t * kvh, LANES), lambda b, pt: (b, 0, 0))
    q_spec = pl.BlockSpec((1, t, q.shape[2]), lambda b, pt: (b, 0, 0))
    vec = pl.BlockSpec((1, DA_DIM), lambda b, pt: (0, 0))
    pages = [page_spec(pg) for pg in range(n_pages)]
    grid_spec = pltpu.PrefetchScalarGridSpec(
        num_scalar_prefetch=1,
        grid=(n,),
        in_specs=[q_spec, new_spec, new_spec] + pages + pages
                 + [vec, vec, vec, vec, pl.BlockSpec((1, LANES), lambda b, pt: (0, 0))],
        out_specs=q_spec,
        scratch_shapes=[pltpu.VMEM((kvh * 4 * t, LANES), BF16), pltpu.VMEM((kvh, t, 2 * LANES), F32)],
    )
    return pl.pallas_call(
        functools.partial(_attn_sample_kernel, n_pages=n_pages, t=t, kvh=kvh, lam_init=lam_init),
        out_shape=jax.ShapeDtypeStruct(q.shape, F32),
        grid_spec=grid_spec,
        compiler_params=_cparams("parallel"),
        name="attn_sample",
    )(page_table, q, k_new, v_new, *([cache_k] * n_pages), *([cache_v] * n_pages), *lams, subln)


def _split3(x):
    hi = x.astype(BF16)
    r1 = x - hi.astype(F32)
    mid = r1.astype(BF16)
    lo = (r1 - mid.astype(F32)).astype(BF16)
    return hi, mid, lo


def _hgrn_chunk(qb, fb, vb, lb, states, group_ones, off_mask, f_scr, k_scr, v_scr, *, sub, valid):
    c_len, w = qb.shape
    hb = w // LANES
    nsub = c_len // sub
    gw = group_ones.shape[0]
    heads = [slice(h * LANES, (h + 1) * LANES) for h in range(hb)]
    f = lb + (1.0 - lb) * jax.nn.sigmoid(fb)
    if valid < c_len:
        f = jnp.where(lax.broadcasted_iota(jnp.int32, (c_len, w), 0) < valid, f, 1.0)
    lf = jnp.log(f)
    kk = 1.0 - f
    qq = _silu(qb)
    vb16 = vb.astype(BF16)
    for h, sl in enumerate(heads):
        f_scr[h] = f[:, sl]
        k_scr[h] = kk[:, sl]
        v_scr[h] = vb[:, sl]

    def row_bcast(ref, r):
        return jnp.concatenate([jnp.broadcast_to(ref[h, pl.ds(r, 1), :], (sub // 2, LANES))
                                for h in range(hb)], axis=1)

    r_i = lax.broadcasted_iota(jnp.int32, (c_len, c_len), 0)
    c_i = lax.broadcasted_iota(jnp.int32, (c_len, c_len), 1)
    tri = (c_i <= r_i).astype(BF16)
    b = functools.reduce(jnp.add, [jnp.dot(tri, part, preferred_element_type=F32) for part in _split3(lf)])
    b_last = b[c_len - 1:c_len, :]
    qe = (qq * jnp.exp(b)).astype(BF16)
    khat = (kk * jnp.exp(b_last - b)).astype(BF16)
    e_last = jnp.exp(b_last)

    if nsub > 1:
        q_parts, k_parts, v_parts = [], [], []
        for i in range(1, nsub):
            lo = i * sub
            b_ref = b[lo - 1:lo, :]
            q_parts.append((qq[lo:lo + sub] * jnp.exp(b[lo:lo + sub] - b_ref)).astype(BF16))
            k_parts.append((kk[:lo] * jnp.exp(b_ref - b[:lo])).astype(BF16))
            v_parts.append(vb16[:lo])
        q_off, k_off, v_off = (jnp.concatenate(p, axis=0) for p in (q_parts, k_parts, v_parts))
        atts = [lax.dot_general(q_off[:, sl], k_off[:, sl], _NT, preferred_element_type=F32) for sl in heads]

    o_inter = [jnp.dot(qe[:, sl], states[h].astype(BF16), preferred_element_type=F32)
               for h, sl in enumerate(heads)]
    upds = [lax.dot_general(khat[:, sl], vb16[:, sl], _TN, preferred_element_type=F32) for sl in heads]

    half = sub // 2
    trow = lax.broadcasted_iota(jnp.int32, (half, w), 0)
    o_diag = []
    for i in range(nsub):
        lo = i * sub
        q_lo, q_hi = qq[lo:lo + half], qq[lo + half:lo + sub]
        g_hi = jnp.where(trow == half - 1, q_hi, 0.0)
        upper = [None] * half
        upper[half - 1] = g_hi * row_bcast(k_scr, lo + sub - 1)
        for s in range(sub - 2, half - 1, -1):
            g_hi = jnp.where(trow == s - half, q_hi, g_hi * row_bcast(f_scr, lo + s + 1))
            upper[s - half] = g_hi * row_bcast(k_scr, lo + s)
        g_lo = jnp.zeros((half, w), F32)
        both = [None] * half
        for s in range(half - 1, -1, -1):
            f_row = row_bcast(f_scr, lo + s + 1)
            k_row = row_bcast(k_scr, lo + s)
            g_hi = g_hi * f_row
            g_lo = jnp.where(trow == s, q_lo, g_lo * f_row)
            both[s] = jnp.concatenate([g_lo * k_row, g_hi * k_row], axis=0)
        p_all = jnp.concatenate(both + upper, axis=0).astype(BF16)
        rsum = jnp.concatenate(
            [jnp.dot(p_all[:, j * gw:(j + 1) * gw], group_ones, preferred_element_type=F32)
             for j in range(w // gw)], axis=1)
        o_lo = jnp.zeros((half, w), F32)
        o_hi = jnp.zeros((half, w), F32)
        for s in range(half):
            v_row = row_bcast(v_scr, lo + s)
            o_lo = o_lo + rsum[s * sub:s * sub + half] * v_row
            o_hi = o_hi + rsum[s * sub + half:(s + 1) * sub] * v_row
        for j in range(half):
            base = half * sub + j * half
            o_hi = o_hi + rsum[base:base + half] * row_bcast(v_scr, lo + half + j)
        o_diag += [o_lo, o_hi]
    o_diag = jnp.concatenate(o_diag, axis=0)

    outs, new_states = [], []
    for h, sl in enumerate(heads):
        o_h = o_inter[h] + o_diag[:, sl]
        if nsub > 1:
            att = jnp.where(off_mask, atts[h], 0.0).astype(BF16)
            o_off = jnp.dot(att, v_off[:, sl], preferred_element_type=F32)
            o_h = o_h + jnp.concatenate([jnp.zeros((sub, LANES), F32), o_off], axis=0)
        outs.append(o_h)
        decay = jnp.broadcast_to(e_last[:, sl], (LANES, LANES)).T
        new_states.append(decay * states[h] + upds[h])
    return outs, new_states


def _hgrn_kernel(q_ref, f_ref, v_ref, go_ref, gb_ref, lb_ref, g_ref, *rest, hb, chunk, sub, has_init):
    if has_init:
        s0_ref, o_ref, s_ref, f_scr, k_scr, v_scr = rest
    else:
        o_ref, s_ref, f_scr, k_scr, v_scr = rest
    tb = pl.program_id(2)

    @pl.when(tb == 0)
    def _():
        if has_init:
            s_ref[...] = s0_ref[...]
        else:
            s_ref[...] = jnp.zeros(s_ref.shape, F32)

    gw = MXU_COLS if (hb * LANES) % MXU_COLS == 0 else LANES
    gr = lax.broadcasted_iota(jnp.int32, (gw, gw), 0) // LANES
    gc = lax.broadcasted_iota(jnp.int32, (gw, gw), 1) // LANES
    group_ones = (gr == gc).astype(BF16)
    nsub = chunk // sub
    off_mask = None
    if nsub > 1:
        shape = ((nsub - 1) * sub, sub * nsub * (nsub - 1) // 2)
        qblk = lax.broadcasted_iota(jnp.int32, shape, 0) // sub + 1
        kcol = lax.broadcasted_iota(jnp.int32, shape, 1)
        off_mask = functools.reduce(jnp.logical_or, [
            (qblk == i) & (kcol >= sub * i * (i - 1) // 2) & (kcol < sub * i * (i + 1) // 2)
            for i in range(1, nsub)])
    t_blk = q_ref.shape[1]

    def run(rows, valid):
        tile = lambda ref: _pad_rows(ref[0, rows, :], chunk)
        outs, new = _hgrn_chunk(tile(q_ref), tile(f_ref), tile(v_ref), lb_ref[...],
                                [s_ref[0, hh] for hh in range(hb)], group_ones, off_mask,
                                f_scr, k_scr, v_scr, sub=sub, valid=valid)
        for hh in range(hb):
            sl = slice(hh * LANES, (hh + 1) * LANES)
            s_ref[0, hh] = new[hh]
            y = _rms(outs[hh][:valid], g_ref[...]) * _silu(go_ref[0, rows, sl])
            o_ref[0, rows, sl] = jax.nn.sigmoid(gb_ref[0, rows, sl]) * y

    if t_blk < chunk:
        run(slice(None), t_blk)
        return

    def body(ci, carry):
        run(pl.ds(pl.multiple_of(ci * chunk, chunk), chunk), chunk)
        return carry

    lax.fori_loop(0, t_blk // chunk, body, 0, unroll=4)


def hgrn(rest, lb, hg_g, s0, layer, *, n, L, tb, hb, chunk, sub):
    d = HG_HEADS * HG_DK
    x3 = rest.reshape(n, L, rest.shape[1])
    nhb = HG_HEADS // hb
    w = hb * LANES
    has_init = s0 is not None

    def col_spec(seg):
        return pl.BlockSpec((1, tb, w), lambda b, h, t: (b, t, seg * (d // w) + h))

    in_specs = [col_spec(0), col_spec(1), col_spec(2), col_spec(3), col_spec(5),
                pl.BlockSpec((1, w), lambda b, h, t: (0, h)),
                pl.BlockSpec((1, LANES), lambda b, h, t: (0, 0))]
    args = [x3, x3, x3, x3, x3, lb, hg_g]
    st_spec = pl.BlockSpec((1, hb, HG_DK, LANES), lambda b, h, t: (b, h, 0, 0))
    if has_init:
        in_specs.append(pl.BlockSpec((1, 1, hb, HG_DK, LANES), lambda b, h, t: (layer, b, h, 0, 0)))
        args.append(s0)
    kernel_fn = functools.partial(_hgrn_kernel, hb=hb, chunk=chunk, sub=sub, has_init=has_init)
    if has_init:
        inner = kernel_fn

        def kernel_fn(q_ref, f_ref, v_ref, go_ref, gb_ref, lb_ref, g_ref, s0_ref, *rest):
            inner(q_ref, f_ref, v_ref, go_ref, gb_ref, lb_ref, g_ref, s0_ref.at[0], *rest)

    return pl.pallas_call(
        kernel_fn,
        out_shape=[jax.ShapeDtypeStruct((n, L, d), F32),
                   jax.ShapeDtypeStruct((n, HG_HEADS, HG_DK, LANES), F32)],
        grid=(n, nhb, L // tb),
        in_specs=in_specs,
        out_specs=[pl.BlockSpec((1, tb, w), lambda b, h, t: (b, t, h)), st_spec],
        scratch_shapes=[pltpu.VMEM((hb, chunk, LANES), F32)] * 3,
        compiler_params=_cparams("parallel", "parallel", "arbitrary"),
        name="hgrn",
    )(*args)


def _out_tail(a, w_ref, x_ref, post_ref, nxt_ref, y_ref, hn_ref):
    t = jnp.dot(a, w_ref[...], preferred_element_type=F32)
    y = x_ref[...] + _rms(t, post_ref[...])
    y_ref[...] = y
    hn_ref[...] = _rms(y, nxt_ref[...]).astype(BF16)


def _outproj_kernel(a_ref, w_ref, x_ref, post_ref, nxt_ref, y_ref, hn_ref):
    _out_tail(a_ref[...].astype(BF16), w_ref, x_ref, post_ref, nxt_ref, y_ref, hn_ref)


def _merge_kernel(ya_ref, yb_ref, ga_ref, w_ref, x_ref, post_ref, nxt_ref, y_ref, hn_ref):
    m = jax.nn.sigmoid(ga_ref[...]) * ya_ref[...].astype(F32) + yb_ref[...]
    _out_tail(m.astype(BF16), w_ref, x_ref, post_ref, nxt_ref, y_ref, hn_ref)


def _out_common(kernel, lead_args, lead_specs, w, x, post, nxt, tm, name):
    rows, d = x.shape
    kdim = w.shape[0]
    vec = pl.BlockSpec((1, d), lambda i: (0, 0))
    row = pl.BlockSpec((tm, d), lambda i: (i, 0))
    return pl.pallas_call(
        kernel,
        out_shape=[jax.ShapeDtypeStruct((rows, d), F32), jax.ShapeDtypeStruct((rows, d), BF16)],
        grid=(rows // tm,),
        in_specs=lead_specs + [pl.BlockSpec((kdim, d), lambda i: (0, 0)), row, vec, vec],
        out_specs=[row, row],
        compiler_params=_cparams("parallel"),
        name=name,
    )(*lead_args, w, x, post, nxt)


def outproj(a, w, x, post, nxt, *, tm):
    spec = pl.BlockSpec((tm, a.shape[1]), lambda i: (i, 0))
    return _out_common(_outproj_kernel, [a], [spec], w, x, post, nxt, tm, "outproj")


def merge(ya, yb, rest, w, x, post, nxt, *, tm):
    d = x.shape[1]
    row = pl.BlockSpec((tm, d), lambda i: (i, 0))
    gate = lambda seg: pl.BlockSpec((tm, d), lambda i: (i, seg))
    return _out_common(_merge_kernel, [ya, yb, rest], [row, row, gate(4)], w, x, post, nxt, tm, "merge")


def _cross_kernel(q_ref, k_ref, v_ref, o_ref, *, nb, heads, native):
    scale = CA_DIM ** -0.5 * LOG2E
    for b in range(nb):
        for h in range(heads):
            sl = slice(h * CA_DIM, (h + 1) * CA_DIM)
            q = q_ref[b, :, sl]
            tq = q.shape[0]
            if tq % BF16_ROWS:
                q = _pad_rows(q.astype(F32), BF16_ROWS * pl.cdiv(tq, BF16_ROWS))
            if native:
                k, v = k_ref[0, b, :, h, :], v_ref[0, b, :, h, :]
            else:
                k, v = k_ref[b, :, sl], v_ref[b, :, sl]
            s = lax.dot_general(q.astype(BF16), k.astype(BF16), _NT, preferred_element_type=F32) * scale
            p = jnp.exp2(s - jnp.max(s, axis=-1, keepdims=True))
            l = jnp.sum(p, axis=-1, keepdims=True)
            o = jnp.dot(p.astype(BF16), v.astype(BF16), preferred_element_type=F32) / l
            o_ref[b, :, sl] = o[:tq].astype(o_ref.dtype)


def cross_attend(q, mk, mv, layer, *, nb, tq):
    n, L, w = q.shape
    native = mk.ndim == 5
    qspec = pl.BlockSpec((nb, tq, w), lambda b, i: (b, i, 0))
    if native:
        kspec = pl.BlockSpec((1, nb) + mk.shape[2:], lambda b, i: (layer, b, 0, 0, 0))
    else:
        kspec = pl.BlockSpec((nb,) + mk.shape[1:], lambda b, i: (b, 0, 0))
    return pl.pallas_call(
        functools.partial(_cross_kernel, nb=nb, heads=w // CA_DIM, native=native),
        out_shape=jax.ShapeDtypeStruct(q.shape, q.dtype),
        grid=(n // nb, L // tq),
        in_specs=[qspec, kspec, kspec],
        out_specs=qspec,
        compiler_params=_cparams("parallel", "arbitrary"),
        name="cross_attend",
    )(q, mk, mv)


def _row(v):
    return v.reshape(1, -1).astype(F32)


def _group_step(x, pos, n, L, layer, lw, lower, attn_fn, s0, mem_k, mem_v, cfg):
    d = x.shape[1]
    tm = cfg["tm"]
    x1, h1 = ffn_half(x, lw["ffn1_pre"], lw["ffn1_post"], lw["mix_pre"], lw["ffn1_w_gu"], lw["ffn1_w_down"],
                      tm=cfg["tm_ffn"], tf=cfg["tf"], emit_next=True)

    tabs = rope_tables(pos)
    period = max(L // tm, 1)
    if L < tm:
        tabs = tuple(jnp.tile(t, (tm // L, 1)) for t in tabs)
    nq = DA_HEADS * 2 * DA_DIM
    nk = DA_KV_HEADS * 2 * DA_DIM
    w_in = lw["w_in"]
    short = L % BF16_ROWS != 0
    (qa,) = project(h1, w_in, 0, nq, tm=tm, tn=cfg["tn"], rope_tabs=tabs, rope_period=period,
                    scale=DA_DIM ** -0.5 * LOG2E, emit_f32=short, emit_bf16=not short)
    kv_out = project(h1, w_in, nq, nk, tm=tm, tn=cfg["tn"], rope_tabs=tabs, rope_period=period,
                     emit_f32=True, emit_bf16=not short)
    vv_out = project(h1, w_in, nq + nk, nk, tm=tm, tn=cfg["tn"], emit_f32=True, emit_bf16=not short)
    k_rows = kv_out[0].reshape(n, L, DA_KV_HEADS, 2 * DA_DIM)
    v_rows = vv_out[0].reshape(n, L, DA_KV_HEADS, 2 * DA_DIM)
    (rest,) = project(h1, w_in, nq + 2 * nk, 6 * d, tm=tm, tn=cfg["tn"], emit_f32=True, emit_bf16=False)

    if short:
        ya = attn_fn(qa.reshape(n, L, nq), k_rows, v_rows)
    else:
        ya = attn_fn(qa.reshape(n, L, nq), kv_out[1].reshape(n, L, nk), vv_out[1].reshape(n, L, nk))
    yb, s_new = hgrn(rest, lower, lw["hg_norm_g"], s0, layer, n=n, L=L, tb=cfg["hg_tb"], hb=cfg["hg_hb"],
                     chunk=cfg["hg_chunk"], sub=cfg["hg_sub"])
    x2, h2 = merge(ya.reshape(n * L, d), yb.reshape(n * L, d), rest, lw["w_out"], x1,
                   lw["mix_post"], lw["ca_pre"], tm=cfg["tm_merge"])

    (qc,) = project(h2, lw["w_cq"], 0, CA_HEADS * CA_DIM, tm=tm, tn=CA_HEADS * CA_DIM,
                    emit_f32=short, emit_bf16=not short)
    oc = cross_attend(qc.reshape(n, L, -1), mem_k, mem_v, layer, nb=cfg["ca_nb"], tq=cfg["ca_tq"])
    x3, _ = outproj(oc.reshape(n * L, -1), lw["w_co"], x2, lw["ca_post"], lw["ffn2_pre"], tm=cfg["tm_merge"])

    x4, _ = ffn_half(x3, lw["ffn2_pre"], lw["ffn2_post"], lw["ffn2_pre"], lw["ffn2_w_gu"], lw["ffn2_w_down"],
                     tm=cfg["tm_ffn"], tf=cfg["tf"], emit_next=False)
    return x4, k_rows, v_rows, s_new


def kernel(x_prompt, x_sample, cache_k, cache_v, state_hgrn, cache_mem_k, cache_mem_v, page_table, mem_prompt, ffn1_pre, ffn1_post, ffn1_w_gu, ffn1_w_down, mix_pre, mix_post, w_in, w_out, lambda_q1, lambda_k1, lambda_q2, lambda_k2, subln_g, hg_norm_g, hg_lb_logits, ca_pre, ca_post, mem_norm_g, w_cq, w_ckv, w_co, ffn2_pre, ffn2_post, ffn2_w_gu, ffn2_w_down):
    n_p, seq, d = x_prompt.shape
    n_s, dec_seq, _ = x_sample.shape
    depth = ffn1_pre.shape[0]
    page = cache_k.shape[2]
    past = page_table.shape[1] * page
    mem_len = mem_prompt.shape[1]
    ca_w = CA_HEADS * CA_DIM

    lower = jnp.cumsum(jax.nn.softmax(hg_lb_logits.astype(F32), axis=0), axis=0)
    xp = x_prompt.reshape(n_p * seq, d)
    xs = x_sample.reshape(n_s * dec_seq, d)
    pos_p = jnp.arange(seq)
    pos_s = past + jnp.arange(dec_seq)

    rows_p, rows_s = n_p * seq, n_s * dec_seq
    cfg_p = dict(tm=min(1024, rows_p), tn=1024, tm_ffn=min(512, rows_p), tf=512, tm_merge=min(256, rows_p),
                 hg_tb=min(512, seq), hg_hb=4, hg_chunk=64, hg_sub=16, ca_nb=1, ca_tq=min(512, seq))
    cfg_s = dict(tm=min(1024, rows_s), tn=1024, tm_ffn=min(512, rows_s), tf=512, tm_merge=min(256, rows_s),
                 hg_tb=dec_seq, hg_hb=16, hg_chunk=BF16_ROWS, hg_sub=BF16_ROWS, ca_nb=min(4, n_s), ca_tq=dec_seq)

    outs = [[] for _ in range(8)]
    for l in range(depth):
        lam_init = 0.8 - 0.6 * math.exp(-0.3 * l)
        lw = {
            "ffn1_pre": _row(ffn1_pre[l]), "ffn1_post": _row(ffn1_post[l]),
            "ffn1_w_gu": ffn1_w_gu[l].astype(BF16), "ffn1_w_down": ffn1_w_down[l].astype(BF16),
            "mix_pre": _row(mix_pre[l]), "mix_post": _row(mix_post[l]),
            "w_in": w_in[l].astype(BF16), "w_out": w_out[l].astype(BF16),
            "hg_norm_g": _row(hg_norm_g[l]),
            "ca_pre": _row(ca_pre[l]), "ca_post": _row(ca_post[l]),
            "w_cq": w_cq[l].astype(BF16), "w_co": w_co[l].astype(BF16),
            "ffn2_pre": _row(ffn2_pre[l]), "ffn2_post": _row(ffn2_post[l]),
            "ffn2_w_gu": ffn2_w_gu[l].astype(BF16), "ffn2_w_down": ffn2_w_down[l].astype(BF16),
        }
        lams = [_row(lambda_q1[l]), _row(lambda_k1[l]), _row(lambda_q2[l]), _row(lambda_k2[l])]
        subln = _row(subln_g[l])
        lb = _row(lower[l])

        rows_m = n_p * mem_len
        hm = rmsnorm_bf16(mem_prompt.reshape(rows_m, d), _row(mem_norm_g[l]), tm=min(512, rows_m))
        w_ckv_l = w_ckv[l].astype(BF16)
        tm_m = min(1024, rows_m)
        mk32, mk16 = project(hm, w_ckv_l, 0, ca_w, tm=tm_m, tn=ca_w, emit_f32=True, emit_bf16=True)
        mv32, mv16 = project(hm, w_ckv_l, ca_w, ca_w, tm=tm_m, tn=ca_w, emit_f32=True, emit_bf16=True)

        attn_p = functools.partial(attn_prompt, lams=lams, subln=subln, lam_init=lam_init,
                                   tq=min(256, seq))
        xp, kp, vp, sp = _group_step(xp, pos_p, n_p, seq, l, lw, lb, attn_p, None,
                                     mk16.reshape(n_p, mem_len, ca_w), mv16.reshape(n_p, mem_len, ca_w), cfg_p)

        def attn_s(q, k, v):
            return attn_sample(page_table, q, k, v, cache_k, cache_v, l, lams, subln, lam_init)

        xs, ks, vs, ss = _group_step(xs, pos_s, n_s, dec_seq, l, lw, lb, attn_s, state_hgrn,
                                     cache_mem_k, cache_mem_v, cfg_s)

        for lst, val in zip(outs, (
                kp, vp, sp, mk32.reshape(n_p, mem_len, CA_HEADS, CA_DIM),
                mv32.reshape(n_p, mem_len, CA_HEADS, CA_DIM), ks, vs, ss)):
            lst.append(val)

    return (xp.reshape(n_p, seq, d), xs.reshape(n_s, dec_seq, d), *[jnp.stack(o) for o in outs])
```

```python
import functools
import math

import jax
import jax.numpy as jnp
from jax import lax
from jax.experimental import pallas as pl
from jax.experimental.pallas import tpu as pltpu

F32 = jnp.float32
BF16 = jnp.bfloat16

EPS = 1e-6
ROPE_THETA = 500000.0
LANES = 128
BF16_ROWS = 16
MXU_COLS = 256
VMEM_LIMIT_BYTES = 56 * 1024 * 1024
LOG2E = math.log2(math.e)

DA_HEADS = 16
DA_KV_HEADS = 8
DA_DIM = 64
ROPE_DIM = DA_DIM // 4
HG_HEADS = 16
HG_DK = 128
CA_HEADS = 4
CA_DIM = 128


def _cparams(*sem):
    return pltpu.CompilerParams(dimension_semantics=sem, vmem_limit_bytes=VMEM_LIMIT_BYTES)


def _rms(x, g):
    ms = jnp.mean(x * x, axis=-1, keepdims=True)
    return (x * lax.rsqrt(ms + EPS)) * g


def _silu(x):
    return x * jax.nn.sigmoid(x)


def _pad_rows(x, rows):
    if x.shape[0] == rows:
        return x
    return jnp.concatenate([x, jnp.zeros((rows - x.shape[0], x.shape[1]), x.dtype)], axis=0)


_NT = (((1,), (1,)), ((), ()))
_TN = (((0,), (0,)), ((), ()))


def _rmsnorm_kernel(x_ref, g_ref, o_ref):
    o_ref[...] = _rms(x_ref[...], g_ref[...]).astype(o_ref.dtype)


def rmsnorm_bf16(x, g, *, tm):
    rows, d = x.shape
    return pl.pallas_call(
        _rmsnorm_kernel,
        out_shape=jax.ShapeDtypeStruct((rows, d), BF16),
        grid=(rows // tm,),
        in_specs=[pl.BlockSpec((tm, d), lambda i: (i, 0)),
                  pl.BlockSpec((1, d), lambda i: (0, 0))],
        out_specs=pl.BlockSpec((tm, d), lambda i: (i, 0)),
        compiler_params=_cparams("parallel"),
        name="rmsnorm",
    )(x, g)


def _ffn_kernel(x_ref, pre_ref, post_ref, nxt_ref, wg_ref, wu_ref, wd_ref, *rest, emit_next):
    if emit_next:
        y_ref, hn_ref, h_scr, acc_scr = rest
    else:
        y_ref, h_scr, acc_scr = rest
    j = pl.program_id(1)

    @pl.when(j == 0)
    def _():
        h_scr[...] = _rms(x_ref[...], pre_ref[...]).astype(BF16)
        acc_scr[...] = jnp.zeros(acc_scr.shape, F32)

    h = h_scr[...]
    g = jnp.dot(h, wg_ref[...], preferred_element_type=F32)
    u = jnp.dot(h, wu_ref[...], preferred_element_type=F32)
    a = (_silu(g) * u).astype(BF16)
    cw = 2 * MXU_COLS
    for c in range(acc_scr.shape[1] // cw):
        sl = slice(c * cw, (c + 1) * cw)
        acc_scr[:, sl] += jnp.dot(a, wd_ref[:, sl], preferred_element_type=F32)

    @pl.when(j == pl.num_programs(1) - 1)
    def _():
        y = x_ref[...] + 0.5 * _rms(acc_scr[...], post_ref[...])
        y_ref[...] = y
        if emit_next:
            hn_ref[...] = _rms(y, nxt_ref[...]).astype(BF16)


def ffn_half(x, pre, post, nxt, w_gu, w_down, *, tm, tf, emit_next):
    rows, d = x.shape
    d_ff = w_down.shape[0]
    nj = d_ff // tf
    out_shape = [jax.ShapeDtypeStruct((rows, d), F32)]
    out_specs = [pl.BlockSpec((tm, d), lambda i, j: (i, 0))]
    if emit_next:
        out_shape.append(jax.ShapeDtypeStruct((rows, d), BF16))
        out_specs.append(pl.BlockSpec((tm, d), lambda i, j: (i, 0)))
    vec = pl.BlockSpec((1, d), lambda i, j: (0, 0))
    res = pl.pallas_call(
        functools.partial(_ffn_kernel, emit_next=emit_next),
        out_shape=out_shape,
        grid=(rows // tm, nj),
        in_specs=[pl.BlockSpec((tm, d), lambda i, j: (i, 0)), vec, vec, vec,
                  pl.BlockSpec((d, tf), lambda i, j: (0, j)),
                  pl.BlockSpec((d, tf), lambda i, j: (0, j + nj)),
                  pl.BlockSpec((tf, d), lambda i, j: (j, 0))],
        out_specs=out_specs,
        scratch_shapes=[pltpu.VMEM((tm, d), BF16), pltpu.VMEM((tm, d), F32)],
        compiler_params=_cparams("parallel", "arbitrary"),
        name="ffn_half",
    )(x, pre, post, nxt, w_gu, w_gu, w_down)
    return res if emit_next else (res[0], None)


def _proj_kernel(h_ref, w_ref, *rest, rope, scale, emit_f32, emit_bf16):
    if rope:
        cos_ref, sa_ref, sb_ref = rest[:3]
        outs = rest[3:]
    else:
        outs = rest
    tn = w_ref.shape[1]

    def emit(val, sl):
        if scale != 1.0:
            val = val * scale
        k = 0
        if emit_f32:
            outs[k][:, sl] = val
            k += 1
        if emit_bf16:
            outs[k][:, sl] = val.astype(BF16)

    y = jnp.dot(h_ref[...], w_ref[...], preferred_element_type=F32)
    if rope:
        cos, sa, sb = cos_ref[...], sa_ref[...], sb_ref[...]
        for gidx in range(tn // LANES):
            sl = slice(gidx * LANES, (gidx + 1) * LANES)
            yg = y[:, sl]
            up = pltpu.roll(yg, LANES - ROPE_DIM // 2, axis=1)
            dn = pltpu.roll(yg, ROPE_DIM // 2, axis=1)
            emit(yg * cos + up * sa + dn * sb, sl)
    else:
        emit(y, slice(None))


def project(h, w, c0, n, *, tm, tn, rope_tabs=None, rope_period=1, scale=1.0,
            emit_f32=True, emit_bf16=False):
    rows, kdim = h.shape
    cb0 = c0 // tn
    rope = rope_tabs is not None
    in_specs = [pl.BlockSpec((tm, kdim), lambda i, j: (i, 0)),
                pl.BlockSpec((kdim, tn), lambda i, j: (0, cb0 + j))]
    args = [h, w]
    if rope:
        tab = pl.BlockSpec((tm, LANES), lambda i, j: (i % rope_period, 0))
        in_specs += [tab, tab, tab]
        args += list(rope_tabs)
    out_shape, out_specs = [], []
    for flag, dt in ((emit_f32, F32), (emit_bf16, BF16)):
        if flag:
            out_shape.append(jax.ShapeDtypeStruct((rows, n), dt))
            out_specs.append(pl.BlockSpec((tm, tn), lambda i, j: (i, j)))
    return pl.pallas_call(
        functools.partial(_proj_kernel, rope=rope, scale=scale, emit_f32=emit_f32, emit_bf16=emit_bf16),
        out_shape=out_shape,
        grid=(rows // tm, n // tn),
        in_specs=in_specs,
        out_specs=out_specs,
        compiler_params=_cparams("parallel", "arbitrary"),
        name="project",
    )(*args)


def rope_tables(pos):
    half = ROPE_DIM // 2
    inv = jnp.float32(ROPE_THETA) ** (-jnp.arange(0, ROPE_DIM, 2, dtype=F32) / ROPE_DIM)
    ang = pos.astype(F32)[:, None] * inv[None, :]
    cos, sin = jnp.cos(ang), jnp.sin(ang)
    r = jnp.arange(LANES) % DA_DIM
    idx = r % half
    lo = (r < half)[None, :]
    hi = ((r >= half) & (r < ROPE_DIM))[None, :]
    cos_t = jnp.where(lo | hi, cos[:, idx], 1.0)
    sa_t = jnp.where(lo, -sin[:, idx], 0.0)
    sb_t = jnp.where(hi, sin[:, idx], 0.0)
    return cos_t, sa_t, sb_t


def _lambda(lq1, lk1, lq2, lk2, lam_init):
    s1 = jnp.sum(lq1 * lk1, axis=-1, keepdims=True)
    s2 = jnp.sum(lq2 * lk2, axis=-1, keepdims=True)
    return jnp.exp(s1) - jnp.exp(s2) + lam_init


def _stack_q(q0, q1):
    q0 = q0.astype(F32)
    q1 = q1.astype(F32)
    lane = lax.broadcasted_iota(jnp.int32, q0.shape, 1)
    first = lane < DA_DIM
    z = jnp.zeros_like(q0)
    return jnp.concatenate([jnp.where(first, q0, z), jnp.where(first, q1, z),
                            jnp.where(first, z, q0), jnp.where(first, z, q1)], axis=0).astype(BF16)


def _lane_tiles(x):
    return [x[:, j * LANES:(j + 1) * LANES] for j in range(x.shape[1] // LANES)]


def _diff_finish(acc, l, lam, t, subln, out_scale):
    o = acc / l
    outs = []
    for g in range(2):
        d = o[g * t:(g + 1) * t] - lam * o[(2 + g) * t:(3 + g) * t]
        outs.append(_rms(d, subln) * out_scale)
    return outs


def _attn_prompt_kernel(q_ref, k_ref, v_ref, lq1, lk1, lq2, lk2, sg_ref, o_ref,
                        qs_scr, s_scr, m_scr, acc_scr, *, tq, kvh, lam_init):
    qi = pl.program_id(1)
    tk = 2 * tq
    n_full = qi // 2
    lam = _lambda(lq1[...], lk1[...], lq2[...], lk2[...], lam_init)
    row = lax.broadcasted_iota(jnp.int32, (4 * tq, tk), 0) % tq + qi * tq
    col = lax.broadcasted_iota(jnp.int32, (4 * tq, tk), 1) + n_full * tk
    visible = col <= row

    def head_lanes(h):
        return slice(h * LANES, (h + 1) * LANES)

    def lane_max(s):
        return functools.reduce(jnp.maximum, _lane_tiles(s))

    def start(h):
        qs_scr[h % 2] = _stack_q(q_ref[0, :, head_lanes(2 * h)], q_ref[0, :, head_lanes(2 * h + 1)])
        m_scr[h % 2] = jnp.full(m_scr.shape[1:], -jnp.inf, F32)

    def scores(h, kp):
        off = pl.multiple_of(kp * tk, tk)
        return lax.dot_general(qs_scr[h % 2], k_ref[0, pl.ds(off, tk), head_lanes(h)], _NT,
                               preferred_element_type=F32)

    def pass1(h, kp):
        s = scores(h, kp)
        s_scr[h % 2, kp] = s
        m_scr[h % 2] = jnp.maximum(m_scr[h % 2], lane_max(s))

    def diag1(h):
        s = jnp.where(visible, scores(h, n_full), -jnp.inf)
        s_scr[h % 2, n_full] = s
        m_row = jnp.max(jnp.maximum(m_scr[h % 2], lane_max(s)), axis=-1, keepdims=True)
        m_scr[h % 2] = jnp.broadcast_to(m_row, m_scr.shape[1:])

    def accumulate(h, s, v):
        m_b = m_scr[h % 2]
        p = jnp.concatenate([jnp.exp2(st - m_b) for st in _lane_tiles(s)], axis=1).astype(BF16)
        v_t = jnp.concatenate([v.astype(F32).T, jnp.ones((BF16_ROWS, v.shape[0]), F32)], axis=0).astype(BF16)
        acc_scr[...] += lax.dot_general(v_t, p, _NT, preferred_element_type=F32)

    def reset2():
        acc_scr[...] = jnp.zeros(acc_scr.shape, F32)

    def pass2(h, kp):
        off = pl.multiple_of(kp * tk, tk)
        accumulate(h, s_scr[h % 2, kp], v_ref[0, pl.ds(off, tk), head_lanes(h)])

    def diag2(h, between=None):
        off = pl.multiple_of(n_full * tk, tk)
        accumulate(h, s_scr[h % 2, n_full], v_ref[0, pl.ds(off, tk), head_lanes(h)])
        if between is not None:
            between()

        acc = acc_scr[...]
        o_t = acc[:LANES] / acc[LANES:LANES + 1]
        for g in range(2):
            d = o_t[:, g * tq:(g + 1) * tq] - lam * o_t[:, (2 + g) * tq:(3 + g) * tq]
            ms = jnp.mean(d * d, axis=0, keepdims=True)
            y = (d * lax.rsqrt(ms + EPS)) * sg_ref[...] * (1.0 - lam_init)
            o_ref[0, :, head_lanes(2 * h + g)] = y.T.astype(o_ref.dtype)

    def loop(fn):
        def body(kp, carry):
            fn(kp)
            return carry
        lax.fori_loop(0, n_full, body, 0)

    start(0)
    loop(functools.partial(pass1, 0))
    diag1(0)
    for h in range(1, kvh):
        start(h)
        reset2()

        def both(kp, h=h):
            pass1(h, kp)
            pass2(h - 1, kp)

        loop(both)
        diag2(h - 1, between=functools.partial(diag1, h))
    reset2()
    loop(functools.partial(pass2, kvh - 1))
    diag2(kvh - 1)


def attn_prompt(q, k, v, lams, subln, lam_init, *, tq):
    n, L, dq = q.shape
    dk = k.shape[2]
    kvh = dk // LANES
    vec = pl.BlockSpec((1, DA_DIM), lambda b, i: (0, 0))
    subln = subln.reshape(LANES, 1)
    return pl.pallas_call(
        functools.partial(_attn_prompt_kernel, tq=tq, kvh=kvh, lam_init=lam_init),
        out_shape=jax.ShapeDtypeStruct(q.shape, BF16),
        grid=(n, L // tq),
        in_specs=[pl.BlockSpec((1, tq, dq), lambda b, i: (b, i, 0)),
                  pl.BlockSpec((1, L, dk), lambda b, i: (b, 0, 0)),
                  pl.BlockSpec((1, L, dk), lambda b, i: (b, 0, 0)),
                  vec, vec, vec, vec,
                  pl.BlockSpec((LANES, 1), lambda b, i: (0, 0))],
        out_specs=pl.BlockSpec((1, tq, dq), lambda b, i: (b, i, 0)),
        scratch_shapes=[pltpu.VMEM((2, 4 * tq, LANES), BF16),
                        pltpu.VMEM((2, L // (2 * tq), 4 * tq, 2 * tq), F32),
                        pltpu.VMEM((2, 4 * tq, LANES), F32),
                        pltpu.VMEM((LANES + BF16_ROWS, 4 * tq), F32)],
        compiler_params=_cparams("parallel", "arbitrary"),
        name="attn_prompt",
    )(q, k, v, *lams, subln)


def _attn_sample_kernel(pt_ref, q_ref, kn_ref, vn_ref, *rest, n_pages, t, kvh, lam_init):
    kp = rest[:n_pages]
    vp = rest[n_pages:2 * n_pages]
    lq1, lk1, lq2, lk2, sg_ref, o_ref, qs_scr, o_scr = rest[2 * n_pages:]
    r = 4 * t
    page = kp[0].shape[2] // kvh
    for h in range(kvh):
        q0 = q_ref[0, :, (2 * h) * LANES:(2 * h + 1) * LANES]
        q1 = q_ref[0, :, (2 * h + 1) * LANES:(2 * h + 2) * LANES]
        qs_scr[h * r:(h + 1) * r, :] = _stack_q(q0, q1)
    lam = _lambda(lq1[...], lk1[...], lq2[...], lk2[...], lam_init)
    row = lax.broadcasted_iota(jnp.int32, (r, LANES), 0) % t
    col = lax.broadcasted_iota(jnp.int32, (r, LANES), 1)
    new_visible = col <= row

    def head(h, carry):
        def rows_of(ref, lead, n_tok):
            return ref[lead + (pl.ds(h, n_tok, stride=kvh), slice(None))]

        qs = qs_scr[pl.ds(pl.multiple_of(h * r, r), r), :]
        kn = _pad_rows(rows_of(kn_ref, (0,), t), LANES).astype(BF16)
        vn = _pad_rows(rows_of(vn_ref, (0,), t), LANES).astype(BF16)
        tiles = [jnp.dot(qs, rows_of(kp[pg], (0, 0), page).T.astype(BF16),
                         preferred_element_type=F32) for pg in range(n_pages)]
        tiles.append(jnp.where(new_visible, lax.dot_general(qs, kn, _NT, preferred_element_type=F32),
                               -jnp.inf))
        m = jnp.max(functools.reduce(jnp.maximum, tiles), axis=-1, keepdims=True)
        ps = [jnp.exp2(st - m) for st in tiles]
        l = jnp.sum(functools.reduce(jnp.add, ps), axis=-1, keepdims=True)
        acc = jnp.dot(ps[n_pages].astype(BF16), vn, preferred_element_type=F32)
        for pg in range(n_pages):
            acc = acc + jnp.dot(ps[pg].astype(BF16), rows_of(vp[pg], (0, 0), page).astype(BF16),
                                preferred_element_type=F32)
        o0, o1 = _diff_finish(acc, l, lam, t, sg_ref[...], 1.0 - lam_init)
        o_scr[h] = jnp.concatenate([o0, o1], axis=1)
        return carry

    lax.fori_loop(0, kvh, head, 0, unroll=8)
    for h in range(kvh):
        o_ref[0, :, 2 * h * LANES:(2 * h + 2) * LANES] = o_scr[h]


def attn_sample(page_table, q, k_new, v_new, cache_k, cache_v, layer, lams, subln, lam_init):
    n, t, _ = q.shape
    kvh = k_new.shape[2]
    n_pages = page_table.shape[1]
    depth, n_pool, page = cache_k.shape[:3]
    k_new, v_new = (a.reshape(n, t * kvh, LANES) for a in (k_new, v_new))
    cache_k, cache_v = (a.reshape(depth, n_pool, page * kvh, LANES) for a in (cache_k, cache_v))

    def page_spec(pg):
        return pl.BlockSpec((1, 1, page * kvh, LANES), lambda b, pt: (layer, pt[b, pg], 0, 0))

    new_spec = pl.BlockSpec((1, t * kvh, LANES), lambda b, pt: (b, 0, 0))
    q_spec = pl.BlockSpec((1, t, q.shape[2]), lambda b, pt: (b, 0, 0))
    vec = pl.BlockSpec((1, DA_DIM), lambda b, pt: (0, 0))
    pages = [page_spec(pg) for pg in range(n_pages)]
    grid_spec = pltpu.PrefetchScalarGridSpec(
        num_scalar_prefetch=1,
        grid=(n,),
        in_specs=[q_spec, new_spec, new_spec] + pages + pages
                 + [vec, vec, vec, vec, pl.BlockSpec((1, LANES), lambda b, pt: (0, 0))],
        out_specs=q_spec,
        scratch_shapes=[pltpu.VMEM((kvh * 4 * t, LANES), BF16), pltpu.VMEM((kvh, t, 2 * LANES), F32)],
    )
    return pl.pallas_call(
        functools.partial(_attn_sample_kernel, n_pages=n_pages, t=t, kvh=kvh, lam_init=lam_init),
        out_shape=jax.ShapeDtypeStruct(q.shape, F32),
        grid_spec=grid_spec,
        compiler_params=_cparams("parallel"),
        name="attn_sample",
    )(page_table, q, k_new, v_new, *([cache_k] * n_pages), *([cache_v] * n_pages), *lams, subln)


def _split3(x):
    hi = x.astype(BF16)
    r1 = x - hi.astype(F32)
    mid = r1.astype(BF16)
    lo = (r1 - mid.astype(F32)).astype(BF16)
    return hi, mid, lo


def _hgrn_chunk(qb, fb, vb, lb, states, group_ones, off_mask, f_scr, k_scr, v_scr, *, sub, valid):
    c_len, w = qb.shape
    hb = w // LANES
    nsub = c_len // sub
    gw = group_ones.shape[0]
    heads = [slice(h * LANES, (h + 1) * LANES) for h in range(hb)]
    f = lb + (1.0 - lb) * jax.nn.sigmoid(fb)
    if valid < c_len:
        f = jnp.where(lax.broadcasted_iota(jnp.int32, (c_len, w), 0) < valid, f, 1.0)
    lf = jnp.log(f)
    kk = 1.0 - f
    qq = _silu(qb)
    vb16 = vb.astype(BF16)
    for h, sl in enumerate(heads):
        f_scr[h] = f[:, sl]
        k_scr[h] = kk[:, sl]
        v_scr[h] = vb[:, sl]

    def row_bcast(ref, r):
        return jnp.concatenate([jnp.broadcast_to(ref[h, pl.ds(r, 1), :], (sub // 2, LANES))
                                for h in range(hb)], axis=1)

    r_i = lax.broadcasted_iota(jnp.int32, (c_len, c_len), 0)
    c_i = lax.broadcasted_iota(jnp.int32, (c_len, c_len), 1)
    tri = (c_i <= r_i).astype(BF16)
    b = functools.reduce(jnp.add, [jnp.dot(tri, part, preferred_element_type=F32) for part in _split3(lf)])
    b_last = b[c_len - 1:c_len, :]
    qe = (qq * jnp.exp(b)).astype(BF16)
    khat = (kk * jnp.exp(b_last - b)).astype(BF16)
    e_last = jnp.exp(b_last)

    if nsub > 1:
        q_parts, k_parts, v_parts = [], [], []
        for i in range(1, nsub):
            lo = i * sub
            b_ref = b[lo - 1:lo, :]
            q_parts.append((qq[lo:lo + sub] * jnp.exp(b[lo:lo + sub] - b_ref)).astype(BF16))
            k_parts.append((kk[:lo] * jnp.exp(b_ref - b[:lo])).astype(BF16))
            v_parts.append(vb16[:lo])
        q_off, k_off, v_off = (jnp.concatenate(p, axis=0) for p in (q_parts, k_parts, v_parts))
        atts = [lax.dot_general(q_off[:, sl], k_off[:, sl], _NT, preferred_element_type=F32) for sl in heads]

    o_inter = [jnp.dot(qe[:, sl], states[h].astype(BF16), preferred_element_type=F32)
               for h, sl in enumerate(heads)]
    upds = [lax.dot_general(khat[:, sl], vb16[:, sl], _TN, preferred_element_type=F32) for sl in heads]

    half = sub // 2
    trow = lax.broadcasted_iota(jnp.int32, (half, w), 0)
    o_diag = []
    for i in range(nsub):
        lo = i * sub
        q_lo, q_hi = qq[lo:lo + half], qq[lo + half:lo + sub]
        g_hi = jnp.where(trow == half - 1, q_hi, 0.0)
        upper = [None] * half
        upper[half - 1] = g_hi * row_bcast(k_scr, lo + sub - 1)
        for s in range(sub - 2, half - 1, -1):
            g_hi = jnp.where(trow == s - half, q_hi, g_hi * row_bcast(f_scr, lo + s + 1))
            upper[s - half] = g_hi * row_bcast(k_scr, lo + s)
        g_lo = jnp.zeros((half, w), F32)
        both = [None] * half
        for s in range(half - 1, -1, -1):
            f_row = row_bcast(f_scr, lo + s + 1)
            k_row = row_bcast(k_scr, lo + s)
            g_hi = g_hi * f_row
            g_lo = jnp.where(trow == s, q_lo, g_lo * f_row)
            both[s] = jnp.concatenate([g_lo * k_row, g_hi * k_row], axis=0)
        p_all = jnp.concatenate(both + upper, axis=0).astype(BF16)
        rsum = jnp.concatenate(
            [jnp.dot(p_all[:, j * gw:(j + 1) * gw], group_ones, preferred_element_type=F32)
             for j in range(w // gw)], axis=1)
        o_lo = jnp.zeros((half, w), F32)
        o_hi = jnp.zeros((half, w), F32)
        for s in range(half):
            v_row = row_bcast(v_scr, lo + s)
            o_lo = o_lo + rsum[s * sub:s * sub + half] * v_row
            o_hi = o_hi + rsum[s * sub + half:(s + 1) * sub] * v_row
        for j in range(half):
            base = half * sub + j * half
            o_hi = o_hi + rsum[base:base + half] * row_bcast(v_scr, lo + half + j)
        o_diag += [o_lo, o_hi]
    o_diag = jnp.concatenate(o_diag, axis=0)

    outs, new_states = [], []
    for h, sl in enumerate(heads):
        o_h = o_inter[h] + o_diag[:, sl]
        if nsub > 1:
            att = jnp.where(off_mask, atts[h], 0.0).astype(BF16)
            o_off = jnp.dot(att, v_off[:, sl], preferred_element_type=F32)
            o_h = o_h + jnp.concatenate([jnp.zeros((sub, LANES), F32), o_off], axis=0)
        outs.append(o_h)
        decay = jnp.broadcast_to(e_last[:, sl], (LANES, LANES)).T
        new_states.append(decay * states[h] + upds[h])
    return outs, new_states


def _hgrn_kernel(q_ref, f_ref, v_ref, go_ref, gb_ref, lb_ref, g_ref, *rest, hb, chunk, sub, has_init):
    if has_init:
        s0_ref, o_ref, s_ref, f_scr, k_scr, v_scr = rest
    else:
        o_ref, s_ref, f_scr, k_scr, v_scr = rest
    tb = pl.program_id(2)

    @pl.when(tb == 0)
    def _():
        if has_init:
            s_ref[...] = s0_ref[...]
        else:
            s_ref[...] = jnp.zeros(s_ref.shape, F32)

    gw = MXU_COLS if (hb * LANES) % MXU_COLS == 0 else LANES
    gr = lax.broadcasted_iota(jnp.int32, (gw, gw), 0) // LANES
    gc = lax.broadcasted_iota(jnp.int32, (gw, gw), 1) // LANES
    group_ones = (gr == gc).astype(BF16)
    nsub = chunk // sub
    off_mask = None
    if nsub > 1:
        shape = ((nsub - 1) * sub, sub * nsub * (nsub - 1) // 2)
        qblk = lax.broadcasted_iota(jnp.int32, shape, 0) // sub + 1
        kcol = lax.broadcasted_iota(jnp.int32, shape, 1)
        off_mask = functools.reduce(jnp.logical_or, [
            (qblk == i) & (kcol >= sub * i * (i - 1) // 2) & (kcol < sub * i * (i + 1) // 2)
            for i in range(1, nsub)])
    t_blk = q_ref.shape[1]

    def run(rows, valid):
        tile = lambda ref: _pad_rows(ref[0, rows, :], chunk)
        outs, new = _hgrn_chunk(tile(q_ref), tile(f_ref), tile(v_ref), lb_ref[...],
                                [s_ref[0, hh] for hh in range(hb)], group_ones, off_mask,
                                f_scr, k_scr, v_scr, sub=sub, valid=valid)
        for hh in range(hb):
            sl = slice(hh * LANES, (hh + 1) * LANES)
            s_ref[0, hh] = new[hh]
            y = _rms(outs[hh][:valid], g_ref[...]) * _silu(go_ref[0, rows, sl])
            o_ref[0, rows, sl] = jax.nn.sigmoid(gb_ref[0, rows, sl]) * y

    if t_blk < chunk:
        run(slice(None), t_blk)
        return

    def body(ci, carry):
        run(pl.ds(pl.multiple_of(ci * chunk, chunk), chunk), chunk)
        return carry

    lax.fori_loop(0, t_blk // chunk, body, 0, unroll=8)


def hgrn(rest, lb, hg_g, s0, layer, *, n, L, tb, hb, chunk, sub):
    d = HG_HEADS * HG_DK
    x3 = rest.reshape(n, L, rest.shape[1])
    nhb = HG_HEADS // hb
    w = hb * LANES
    has_init = s0 is not None

    def col_spec(seg):
        return pl.BlockSpec((1, tb, w), lambda b, h, t: (b, t, seg * (d // w) + h))

    in_specs = [col_spec(0), col_spec(1), col_spec(2), col_spec(3), col_spec(5),
                pl.BlockSpec((1, w), lambda b, h, t: (0, h)),
                pl.BlockSpec((1, LANES), lambda b, h, t: (0, 0))]
    args = [x3, x3, x3, x3, x3, lb, hg_g]
    st_spec = pl.BlockSpec((1, hb, HG_DK, LANES), lambda b, h, t: (b, h, 0, 0))
    if has_init:
        in_specs.append(pl.BlockSpec((1, 1, hb, HG_DK, LANES), lambda b, h, t: (layer, b, h, 0, 0)))
        args.append(s0)
    kernel_fn = functools.partial(_hgrn_kernel, hb=hb, chunk=chunk, sub=sub, has_init=has_init)
    if has_init:
        inner = kernel_fn

        def kernel_fn(q_ref, f_ref, v_ref, go_ref, gb_ref, lb_ref, g_ref, s0_ref, *rest):
            inner(q_ref, f_ref, v_ref, go_ref, gb_ref, lb_ref, g_ref, s0_ref.at[0], *rest)

    return pl.pallas_call(
        kernel_fn,
        out_shape=[jax.ShapeDtypeStruct((n, L, d), F32),
                   jax.ShapeDtypeStruct((n, HG_HEADS, HG_DK, LANES), F32)],
        grid=(n, nhb, L // tb),
        in_specs=in_specs,
        out_specs=[pl.BlockSpec((1, tb, w), lambda b, h, t: (b, t, h)), st_spec],
        scratch_shapes=[pltpu.VMEM((hb, chunk, LANES), F32)] * 3,
        compiler_params=_cparams("parallel", "parallel", "arbitrary"),
        name="hgrn",
    )(*args)


def _out_tail(a, w_ref, x_ref, post_ref, nxt_ref, y_ref, hn_ref):
    t = jnp.dot(a, w_ref[...], preferred_element_type=F32)
    y = x_ref[...] + _rms(t, post_ref[...])
    y_ref[...] = y
    hn_ref[...] = _rms(y, nxt_ref[...]).astype(BF16)


def _outproj_kernel(a_ref, w_ref, x_ref, post_ref, nxt_ref, y_ref, hn_ref):
    _out_tail(a_ref[...].astype(BF16), w_ref, x_ref, post_ref, nxt_ref, y_ref, hn_ref)


def _merge_kernel(ya_ref, yb_ref, ga_ref, w_ref, x_ref, post_ref, nxt_ref, y_ref, hn_ref):
    m = jax.nn.sigmoid(ga_ref[...]) * ya_ref[...].astype(F32) + yb_ref[...]
    _out_tail(m.astype(BF16), w_ref, x_ref, post_ref, nxt_ref, y_ref, hn_ref)


def _out_common(kernel, lead_args, lead_specs, w, x, post, nxt, tm, name):
    rows, d = x.shape
    kdim = w.shape[0]
    vec = pl.BlockSpec((1, d), lambda i: (0, 0))
    row = pl.BlockSpec((tm, d), lambda i: (i, 0))
    return pl.pallas_call(
        kernel,
        out_shape=[jax.ShapeDtypeStruct((rows, d), F32), jax.ShapeDtypeStruct((rows, d), BF16)],
        grid=(rows // tm,),
        in_specs=lead_specs + [pl.BlockSpec((kdim, d), lambda i: (0, 0)), row, vec, vec],
        out_specs=[row, row],
        compiler_params=_cparams("parallel"),
        name=name,
    )(*lead_args, w, x, post, nxt)


def outproj(a, w, x, post, nxt, *, tm):
    spec = pl.BlockSpec((tm, a.shape[1]), lambda i: (i, 0))
    return _out_common(_outproj_kernel, [a], [spec], w, x, post, nxt, tm, "outproj")


def merge(ya, yb, rest, w, x, post, nxt, *, tm):
    d = x.shape[1]
    row = pl.BlockSpec((tm, d), lambda i: (i, 0))
    gate = lambda seg: pl.BlockSpec((tm, d), lambda i: (i, seg))
    return _out_common(_merge_kernel, [ya, yb, rest], [row, row, gate(4)], w, x, post, nxt, tm, "merge")


def _cross_kernel(q_ref, k_ref, v_ref, o_ref, *, nb, heads, native):
    scale = CA_DIM ** -0.5 * LOG2E
    for b in range(nb):
        for h in range(heads):
            sl = slice(h * CA_DIM, (h + 1) * CA_DIM)
            q = q_ref[b, :, sl]
            tq = q.shape[0]
            if tq % BF16_ROWS:
                q = _pad_rows(q.astype(F32), BF16_ROWS * pl.cdiv(tq, BF16_ROWS))
            if native:
                k, v = k_ref[0, b, :, h, :], v_ref[0, b, :, h, :]
            else:
                k, v = k_ref[b, :, sl], v_ref[b, :, sl]
            s = lax.dot_general(q.astype(BF16), k.astype(BF16), _NT, preferred_element_type=F32) * scale
            p = jnp.exp2(s - jnp.max(s, axis=-1, keepdims=True))
            l = jnp.sum(p, axis=-1, keepdims=True)
            o = jnp.dot(p.astype(BF16), v.astype(BF16), preferred_element_type=F32) / l
            o_ref[b, :, sl] = o[:tq].astype(o_ref.dtype)


def cross_attend(q, mk, mv, layer, *, nb, tq):
    n, L, w = q.shape
    native = mk.ndim == 5
    qspec = pl.BlockSpec((nb, tq, w), lambda b, i: (b, i, 0))
    if native:
        kspec = pl.BlockSpec((1, nb) + mk.shape[2:], lambda b, i: (layer, b, 0, 0, 0))
    else:
        kspec = pl.BlockSpec((nb,) + mk.shape[1:], lambda b, i: (b, 0, 0))
    return pl.pallas_call(
        functools.partial(_cross_kernel, nb=nb, heads=w // CA_DIM, native=native),
        out_shape=jax.ShapeDtypeStruct(q.shape, q.dtype),
        grid=(n // nb, L // tq),
        in_specs=[qspec, kspec, kspec],
        out_specs=qspec,
        compiler_params=_cparams("parallel", "arbitrary"),
        name="cross_attend",
    )(q, mk, mv)


def _row(v):
    return v.reshape(1, -1).astype(F32)


def _group_step(x, pos, n, L, layer, lw, lower, attn_fn, s0, mem_k, mem_v, cfg):
    d = x.shape[1]
    tm = cfg["tm"]
    x1, h1 = ffn_half(x, lw["ffn1_pre"], lw["ffn1_post"], lw["mix_pre"], lw["ffn1_w_gu"], lw["ffn1_w_down"],
                      tm=cfg["tm_ffn"], tf=cfg["tf"], emit_next=True)

    tabs = rope_tables(pos)
    period = max(L // tm, 1)
    if L < tm:
        tabs = tuple(jnp.tile(t, (tm // L, 1)) for t in tabs)
    nq = DA_HEADS * 2 * DA_DIM
    nk = DA_KV_HEADS * 2 * DA_DIM
    w_in = lw["w_in"]
    short = L % BF16_ROWS != 0
    (qa,) = project(h1, w_in, 0, nq, tm=tm, tn=cfg["tn"], rope_tabs=tabs, rope_period=period,
                    scale=DA_DIM ** -0.5 * LOG2E, emit_f32=short, emit_bf16=not short)
    kv_out = project(h1, w_in, nq, nk, tm=tm, tn=cfg["tn"], rope_tabs=tabs, rope_period=period,
                     emit_f32=True, emit_bf16=not short)
    vv_out = project(h1, w_in, nq + nk, nk, tm=tm, tn=cfg["tn"], emit_f32=True, emit_bf16=not short)
    k_rows = kv_out[0].reshape(n, L, DA_KV_HEADS, 2 * DA_DIM)
    v_rows = vv_out[0].reshape(n, L, DA_KV_HEADS, 2 * DA_DIM)
    (rest,) = project(h1, w_in, nq + 2 * nk, 6 * d, tm=tm, tn=cfg["tn"], emit_f32=True, emit_bf16=False)

    if short:
        ya = attn_fn(qa.reshape(n, L, nq), k_rows, v_rows)
    else:
        ya = attn_fn(qa.reshape(n, L, nq), kv_out[1].reshape(n, L, nk), vv_out[1].reshape(n, L, nk))
    yb, s_new = hgrn(rest, lower, lw["hg_norm_g"], s0, layer, n=n, L=L, tb=cfg["hg_tb"], hb=cfg["hg_hb"],
                     chunk=cfg["hg_chunk"], sub=cfg["hg_sub"])
    x2, h2 = merge(ya.reshape(n * L, d), yb.reshape(n * L, d), rest, lw["w_out"], x1,
                   lw["mix_post"], lw["ca_pre"], tm=cfg["tm_merge"])

    (qc,) = project(h2, lw["w_cq"], 0, CA_HEADS * CA_DIM, tm=tm, tn=CA_HEADS * CA_DIM,
                    emit_f32=short, emit_bf16=not short)
    oc = cross_attend(qc.reshape(n, L, -1), mem_k, mem_v, layer, nb=cfg["ca_nb"], tq=cfg["ca_tq"])
    x3, _ = outproj(oc.reshape(n * L, -1), lw["w_co"], x2, lw["ca_post"], lw["ffn2_pre"], tm=cfg["tm_merge"])

    x4, _ = ffn_half(x3, lw["ffn2_pre"], lw["ffn2_post"], lw["ffn2_pre"], lw["ffn2_w_gu"], lw["ffn2_w_down"],
                     tm=cfg["tm_ffn"], tf=cfg["tf"], emit_next=False)
    return x4, k_rows, v_rows, s_new


def kernel(x_prompt, x_sample, cache_k, cache_v, state_hgrn, cache_mem_k, cache_mem_v, page_table, mem_prompt, ffn1_pre, ffn1_post, ffn1_w_gu, ffn1_w_down, mix_pre, mix_post, w_in, w_out, lambda_q1, lambda_k1, lambda_q2, lambda_k2, subln_g, hg_norm_g, hg_lb_logits, ca_pre, ca_post, mem_norm_g, w_cq, w_ckv, w_co, ffn2_pre, ffn2_post, ffn2_w_gu, ffn2_w_down):
    n_p, seq, d = x_prompt.shape
    n_s, dec_seq, _ = x_sample.shape
    depth = ffn1_pre.shape[0]
    page = cache_k.shape[2]
    past = page_table.shape[1] * page
    mem_len = mem_prompt.shape[1]
    ca_w = CA_HEADS * CA_DIM

    lower = jnp.cumsum(jax.nn.softmax(hg_lb_logits.astype(F32), axis=0), axis=0)
    xp = x_prompt.reshape(n_p * seq, d)
    xs = x_sample.reshape(n_s * dec_seq, d)
    pos_p = jnp.arange(seq)
    pos_s = past + jnp.arange(dec_seq)

    rows_p, rows_s = n_p * seq, n_s * dec_seq
    cfg_p = dict(tm=min(1024, rows_p), tn=1024, tm_ffn=min(512, rows_p), tf=512, tm_merge=min(256, rows_p),
                 hg_tb=min(512, seq), hg_hb=4, hg_chunk=64, hg_sub=16, ca_nb=1, ca_tq=min(512, seq))
    cfg_s = dict(tm=min(1024, rows_s), tn=1024, tm_ffn=min(512, rows_s), tf=512, tm_merge=min(256, rows_s),
                 hg_tb=dec_seq, hg_hb=16, hg_chunk=BF16_ROWS, hg_sub=BF16_ROWS, ca_nb=min(4, n_s), ca_tq=dec_seq)

    outs = [[] for _ in range(8)]
    for l in range(depth):
        lam_init = 0.8 - 0.6 * math.exp(-0.3 * l)
        lw = {
            "ffn1_pre": _row(ffn1_pre[l]), "ffn1_post": _row(ffn1_post[l]),
            "ffn1_w_gu": ffn1_w_gu[l].astype(BF16), "ffn1_w_down": ffn1_w_down[l].astype(BF16),
            "mix_pre": _row(mix_pre[l]), "mix_post": _row(mix_post[l]),
            "w_in": w_in[l].astype(BF16), "w_out": w_out[l].astype(BF16),
            "hg_norm_g": _row(hg_norm_g[l]),
            "ca_pre": _row(ca_pre[l]), "ca_post": _row(ca_post[l]),
            "w_cq": w_cq[l].astype(BF16), "w_co": w_co[l].astype(BF16),
            "ffn2_pre": _row(ffn2_pre[l]), "ffn2_post": _row(ffn2_post[l]),
            "ffn2_w_gu": ffn2_w_gu[l].astype(BF16), "ffn2_w_down": ffn2_w_down[l].astype(BF16),
        }
        lams = [_row(lambda_q1[l]), _row(lambda_k1[l]), _row(lambda_q2[l]), _row(lambda_k2[l])]
        subln = _row(subln_g[l])
        lb = _row(lower[l])

        rows_m = n_p * mem_len
        hm = rmsnorm_bf16(mem_prompt.reshape(rows_m, d), _row(mem_norm_g[l]), tm=min(512, rows_m))
        w_ckv_l = w_ckv[l].astype(BF16)
        tm_m = min(1024, rows_m)
        mk32, mk16 = project(hm, w_ckv_l, 0, ca_w, tm=tm_m, tn=ca_w, emit_f32=True, emit_bf16=True)
        mv32, mv16 = project(hm, w_ckv_l, ca_w, ca_w, tm=tm_m, tn=ca_w, emit_f32=True, emit_bf16=True)

        attn_p = functools.partial(attn_prompt, lams=lams, subln=subln, lam_init=lam_init,
                                   tq=min(256, seq))
        xp, kp, vp, sp = _group_step(xp, pos_p, n_p, seq, l, lw, lb, attn_p, None,
                                     mk16.reshape(n_p, mem_len, ca_w), mv16.reshape(n_p, mem_len, ca_w), cfg_p)

        def attn_s(q, k, v):
            return attn_sample(page_table, q, k, v, cache_k, cache_v, l, lams, subln, lam_init)

        xs, ks, vs, ss = _group_step(xs, pos_s, n_s, dec_seq, l, lw, lb, attn_s, state_hgrn,
                                     cache_mem_k, cache_mem_v, cfg_s)

        for lst, val in zip(outs, (
                kp, vp, sp, mk32.reshape(n_p, mem_len, CA_HEADS, CA_DIM),
                mv32.reshape(n_p, mem_len, CA_HEADS, CA_DIM), ks, vs, ss)):
            lst.append(val)

    return (xp.reshape(n_p, seq, d), xs.reshape(n_s, dec_seq, d), *[jnp.stack(o) for o in outs])
```

```python
import functools
import math

import jax
import jax.numpy as jnp
from jax import lax
from jax.experimental import pallas as pl
from jax.experimental.pallas import tpu as pltpu

F32 = jnp.float32
BF16 = jnp.bfloat16

EPS = 1e-6
ROPE_THETA = 500000.0
LANES = 128
BF16_ROWS = 16
MXU_COLS = 256
VMEM_LIMIT_BYTES = 56 * 1024 * 1024
LOG2E = math.log2(math.e)

DA_HEADS = 16
DA_KV_HEADS = 8
DA_DIM = 64
ROPE_DIM = DA_DIM // 4
HG_HEADS = 16
HG_DK = 128
CA_HEADS = 4
CA_DIM = 128


def _cparams(*sem):
    return pltpu.CompilerParams(dimension_semantics=sem, vmem_limit_bytes=VMEM_LIMIT_BYTES)


def _rms(x, g):
    ms = jnp.mean(x * x, axis=-1, keepdims=True)
    return (x * lax.rsqrt(ms + EPS)) * g


def _silu(x):
    return x * jax.nn.sigmoid(x)


def _pad_rows(x, rows):
    if x.shape[0] == rows:
        return x
    return jnp.concatenate([x, jnp.zeros((rows - x.shape[0], x.shape[1]), x.dtype)], axis=0)


_NT = (((1,), (1,)), ((), ()))
_TN = (((0,), (0,)), ((), ()))


def _rmsnorm_kernel(x_ref, g_ref, o_ref):
    o_ref[...] = _rms(x_ref[...], g_ref[...]).astype(o_ref.dtype)


def rmsnorm_bf16(x, g, *, tm):
    rows, d = x.shape
    return pl.pallas_call(
        _rmsnorm_kernel,
        out_shape=jax.ShapeDtypeStruct((rows, d), BF16),
        grid=(rows // tm,),
        in_specs=[pl.BlockSpec((tm, d), lambda i: (i, 0)),
                  pl.BlockSpec((1, d), lambda i: (0, 0))],
        out_specs=pl.BlockSpec((tm, d), lambda i: (i, 0)),
        compiler_params=_cparams("parallel"),
        name="rmsnorm",
    )(x, g)


def _ffn_kernel(x_ref, pre_ref, post_ref, nxt_ref, wg_ref, wu_ref, wd_ref, *rest, emit_next):
    if emit_next:
        y_ref, hn_ref, h_scr, acc_scr = rest
    else:
        y_ref, h_scr, acc_scr = rest
    j = pl.program_id(1)

    @pl.when(j == 0)
    def _():
        h_scr[...] = _rms(x_ref[...], pre_ref[...]).astype(BF16)
        acc_scr[...] = jnp.zeros(acc_scr.shape, F32)

    h = h_scr[...]
    g = jnp.dot(h, wg_ref[...], preferred_element_type=F32)
    u = jnp.dot(h, wu_ref[...], preferred_element_type=F32)
    a = (_silu(g) * u).astype(BF16)
    cw = 2 * MXU_COLS
    for c in range(acc_scr.shape[1] // cw):
        sl = slice(c * cw, (c + 1) * cw)
        acc_scr[:, sl] += jnp.dot(a, wd_ref[:, sl], preferred_element_type=F32)

    @pl.when(j == pl.num_programs(1) - 1)
    def _():
        y = x_ref[...] + 0.5 * _rms(acc_scr[...], post_ref[...])
        y_ref[...] = y
        if emit_next:
            hn_ref[...] = _rms(y, nxt_ref[...]).astype(BF16)


def ffn_half(x, pre, post, nxt, w_gu, w_down, *, tm, tf, emit_next):
    rows, d = x.shape
    d_ff = w_down.shape[0]
    nj = d_ff // tf
    out_shape = [jax.ShapeDtypeStruct((rows, d), F32)]
    out_specs = [pl.BlockSpec((tm, d), lambda i, j: (i, 0))]
    if emit_next:
        out_shape.append(jax.ShapeDtypeStruct((rows, d), BF16))
        out_specs.append(pl.BlockSpec((tm, d), lambda i, j: (i, 0)))
    vec = pl.BlockSpec((1, d), lambda i, j: (0, 0))
    res = pl.pallas_call(
        functools.partial(_ffn_kernel, emit_next=emit_next),
        out_shape=out_shape,
        grid=(rows // tm, nj),
        in_specs=[pl.BlockSpec((tm, d), lambda i, j: (i, 0)), vec, vec, vec,
                  pl.BlockSpec((d, tf), lambda i, j: (0, j)),
                  pl.BlockSpec((d, tf), lambda i, j: (0, j + nj)),
                  pl.BlockSpec((tf, d), lambda i, j: (j, 0))],
        out_specs=out_specs,
        scratch_shapes=[pltpu.VMEM((tm, d), BF16), pltpu.VMEM((tm, d), F32)],
        compiler_params=_cparams("parallel", "arbitrary"),
        name="ffn_half",
    )(x, pre, post, nxt, w_gu, w_gu, w_down)
    return res if emit_next else (res[0], None)


def _proj_kernel(h_ref, w_ref, *rest, rope, scale, emit_f32, emit_bf16):
    if rope:
        cos_ref, sa_ref, sb_ref = rest[:3]
        outs = rest[3:]
    else:
        outs = rest
    tn = w_ref.shape[1]

    def emit(val, sl):
        if scale != 1.0:
            val = val * scale
        k = 0
        if emit_f32:
            outs[k][:, sl] = val
            k += 1
        if emit_bf16:
            outs[k][:, sl] = val.astype(BF16)

    y = jnp.dot(h_ref[...], w_ref[...], preferred_element_type=F32)
    if rope:
        cos, sa, sb = cos_ref[...], sa_ref[...], sb_ref[...]
        for gidx in range(tn // LANES):
            sl = slice(gidx * LANES, (gidx + 1) * LANES)
            yg = y[:, sl]
            up = pltpu.roll(yg, LANES - ROPE_DIM // 2, axis=1)
            dn = pltpu.roll(yg, ROPE_DIM // 2, axis=1)
            emit(yg * cos + up * sa + dn * sb, sl)
    else:
        emit(y, slice(None))


def project(h, w, c0, n, *, tm, tn, rope_tabs=None, rope_period=1, scale=1.0,
            emit_f32=True, emit_bf16=False):
    rows, kdim = h.shape
    cb0 = c0 // tn
    rope = rope_tabs is not None
    in_specs = [pl.BlockSpec((tm, kdim), lambda i, j: (i, 0)),
                pl.BlockSpec((kdim, tn), lambda i, j: (0, cb0 + j))]
    args = [h, w]
    if rope:
        tab = pl.BlockSpec((tm, LANES), lambda i, j: (i % rope_period, 0))
        in_specs += [tab, tab, tab]
        args += list(rope_tabs)
    out_shape, out_specs = [], []
    for flag, dt in ((emit_f32, F32), (emit_bf16, BF16)):
        if flag:
            out_shape.append(jax.ShapeDtypeStruct((rows, n), dt))
            out_specs.append(pl.BlockSpec((tm, tn), lambda i, j: (i, j)))
    return pl.pallas_call(
        functools.partial(_proj_kernel, rope=rope, scale=scale, emit_f32=emit_f32, emit_bf16=emit_bf16),
        out_shape=out_shape,
        grid=(rows // tm, n // tn),
        in_specs=in_specs,
        out_specs=out_specs,
        compiler_params=_cparams("parallel", "arbitrary"),
        name="project",
    )(*args)


def rope_tables(pos):
    half = ROPE_DIM // 2
    inv = jnp.float32(ROPE_THETA) ** (-jnp.arange(0, ROPE_DIM, 2, dtype=F32) / ROPE_DIM)
    ang = pos.astype(F32)[:, None] * inv[None, :]
    cos, sin = jnp.cos(ang), jnp.sin(ang)
    r = jnp.arange(LANES) % DA_DIM
    idx = r % half
    lo = (r < half)[None, :]
    hi = ((r >= half) & (r < ROPE_DIM))[None, :]
    cos_t = jnp.where(lo | hi, cos[:, idx], 1.0)
    sa_t = jnp.where(lo, -sin[:, idx], 0.0)
    sb_t = jnp.where(hi, sin[:, idx], 0.0)
    return cos_t, sa_t, sb_t


def _lambda(lq1, lk1, lq2, lk2, lam_init):
    s1 = jnp.sum(lq1 * lk1, axis=-1, keepdims=True)
    s2 = jnp.sum(lq2 * lk2, axis=-1, keepdims=True)
    return jnp.exp(s1) - jnp.exp(s2) + lam_init


def _stack_q(q0, q1):
    q0 = q0.astype(F32)
    q1 = q1.astype(F32)
    lane = lax.broadcasted_iota(jnp.int32, q0.shape, 1)
    first = lane < DA_DIM
    z = jnp.zeros_like(q0)
    return jnp.concatenate([jnp.where(first, q0, z), jnp.where(first, q1, z),
                            jnp.where(first, z, q0), jnp.where(first, z, q1)], axis=0).astype(BF16)


def _lane_tiles(x):
    return [x[:, j * LANES:(j + 1) * LANES] for j in range(x.shape[1] // LANES)]


def _diff_finish(acc, l, lam, t, subln, out_scale):
    o = acc / l
    outs = []
    for g in range(2):
        d = o[g * t:(g + 1) * t] - lam * o[(2 + g) * t:(3 + g) * t]
        outs.append(_rms(d, subln) * out_scale)
    return outs


def _attn_prompt_kernel(q_ref, k_ref, v_ref, lq1, lk1, lq2, lk2, sg_ref, o_ref,
                        qs_scr, s_scr, m_scr, acc_scr, *, tq, kvh, lam_init):
    qi = pl.program_id(1)
    tk = 2 * tq
    n_full = qi // 2
    lam = _lambda(lq1[...], lk1[...], lq2[...], lk2[...], lam_init)
    row = lax.broadcasted_iota(jnp.int32, (4 * tq, tk), 0) % tq + qi * tq
    col = lax.broadcasted_iota(jnp.int32, (4 * tq, tk), 1) + n_full * tk
    visible = col <= row

    def head_lanes(h):
        return slice(h * LANES, (h + 1) * LANES)

    def lane_max(s):
        return functools.reduce(jnp.maximum, _lane_tiles(s))

    def start(h):
        qs_scr[h % 2] = _stack_q(q_ref[0, :, head_lanes(2 * h)], q_ref[0, :, head_lanes(2 * h + 1)])
        m_scr[h % 2] = jnp.full(m_scr.shape[1:], -jnp.inf, F32)

    def scores(h, kp):
        off = pl.multiple_of(kp * tk, tk)
        return lax.dot_general(qs_scr[h % 2], k_ref[0, pl.ds(off, tk), head_lanes(h)], _NT,
                               preferred_element_type=F32)

    def pass1(h, kp):
        s = scores(h, kp)
        s_scr[h % 2, kp] = s
        m_scr[h % 2] = jnp.maximum(m_scr[h % 2], lane_max(s))

    def diag1(h):
        s = jnp.where(visible, scores(h, n_full), -jnp.inf)
        s_scr[h % 2, n_full] = s
        m_row = jnp.max(jnp.maximum(m_scr[h % 2], lane_max(s)), axis=-1, keepdims=True)
        m_scr[h % 2] = jnp.broadcast_to(m_row, m_scr.shape[1:])

    def accumulate(h, s, v):
        m_b = m_scr[h % 2]
        p = jnp.concatenate([jnp.exp2(st - m_b) for st in _lane_tiles(s)], axis=1).astype(BF16)
        v_t = jnp.concatenate([v.astype(F32).T, jnp.ones((BF16_ROWS, v.shape[0]), F32)], axis=0).astype(BF16)
        acc_scr[...] += lax.dot_general(v_t, p, _NT, preferred_element_type=F32)

    def reset2():
        acc_scr[...] = jnp.zeros(acc_scr.shape, F32)

    def pass2(h, kp):
        off = pl.multiple_of(kp * tk, tk)
        accumulate(h, s_scr[h % 2, kp], v_ref[0, pl.ds(off, tk), head_lanes(h)])

    def diag2(h, between=None):
        off = pl.multiple_of(n_full * tk, tk)
        accumulate(h, s_scr[h % 2, n_full], v_ref[0, pl.ds(off, tk), head_lanes(h)])
        if between is not None:
            between()

        acc = acc_scr[...]
        o_t = acc[:LANES] / acc[LANES:LANES + 1]
        for g in range(2):
            d = o_t[:, g * tq:(g + 1) * tq] - lam * o_t[:, (2 + g) * tq:(3 + g) * tq]
            ms = jnp.mean(d * d, axis=0, keepdims=True)
            y = (d * lax.rsqrt(ms + EPS)) * sg_ref[...] * (1.0 - lam_init)
            o_ref[0, :, head_lanes(2 * h + g)] = y.T.astype(o_ref.dtype)

    def loop(fn):
        def body(kp, carry):
            fn(kp)
            return carry
        lax.fori_loop(0, n_full, body, 0)

    start(0)
    loop(functools.partial(pass1, 0))
    diag1(0)
    for h in range(1, kvh):
        start(h)
        reset2()

        def both(kp, h=h):
            pass1(h, kp)
            pass2(h - 1, kp)

        loop(both)
        diag2(h - 1, between=functools.partial(diag1, h))
    reset2()
    loop(functools.partial(pass2, kvh - 1))
    diag2(kvh - 1)


def attn_prompt(q, k, v, lams, subln, lam_init, *, tq):
    n, L, dq = q.shape
    dk = k.shape[2]
    kvh = dk // LANES
    vec = pl.BlockSpec((1, DA_DIM), lambda b, i: (0, 0))
    subln = subln.reshape(LANES, 1)
    return pl.pallas_call(
        functools.partial(_attn_prompt_kernel, tq=tq, kvh=kvh, lam_init=lam_init),
        out_shape=jax.ShapeDtypeStruct(q.shape, BF16),
        grid=(n, L // tq),
        in_specs=[pl.BlockSpec((1, tq, dq), lambda b, i: (b, i, 0)),
                  pl.BlockSpec((1, L, dk), lambda b, i: (b, 0, 0)),
                  pl.BlockSpec((1, L, dk), lambda b, i: (b, 0, 0)),
                  vec, vec, vec, vec,
                  pl.BlockSpec((LANES, 1), lambda b, i: (0, 0))],
        out_specs=pl.BlockSpec((1, tq, dq), lambda b, i: (b, i, 0)),
        scratch_shapes=[pltpu.VMEM((2, 4 * tq, LANES), BF16),
                        pltpu.VMEM((2, L // (2 * tq), 4 * tq, 2 * tq), F32),
                        pltpu.VMEM((2, 4 * tq, LANES), F32),
                        pltpu.VMEM((LANES + BF16_ROWS, 4 * tq), F32)],
        compiler_params=_cparams("parallel", "arbitrary"),
        name="attn_prompt",
    )(q, k, v, *lams, subln)


def _attn_sample_kernel(pt_ref, q_ref, kn_ref, vn_ref, *rest, n_pages, t, kvh, lam_init):
    kp = rest[:n_pages]
    vp = rest[n_pages:2 * n_pages]
    lq1, lk1, lq2, lk2, sg_ref, o_ref, qs_scr, o_scr = rest[2 * n_pages:]
    r = 4 * t
    page = kp[0].shape[2] // kvh
    for h in range(kvh):
        q0 = q_ref[0, :, (2 * h) * LANES:(2 * h + 1) * LANES]
        q1 = q_ref[0, :, (2 * h + 1) * LANES:(2 * h + 2) * LANES]
        qs_scr[h * r:(h + 1) * r, :] = _stack_q(q0, q1)
    lam = _lambda(lq1[...], lk1[...], lq2[...], lk2[...], lam_init)
    row = lax.broadcasted_iota(jnp.int32, (r, LANES), 0) % t
    col = lax.broadcasted_iota(jnp.int32, (r, LANES), 1)
    new_visible = col <= row

    def head(h, carry):
        def rows_of(ref, lead, n_tok):
            return ref[lead + (pl.ds(h, n_tok, stride=kvh), slice(None))]

        qs = qs_scr[pl.ds(pl.multiple_of(h * r, r), r), :]
        kn = _pad_rows(rows_of(kn_ref, (0,), t), LANES).astype(BF16)
        vn = _pad_rows(rows_of(vn_ref, (0,), t), LANES).astype(BF16)
        tiles = [jnp.dot(qs, rows_of(kp[pg], (0, 0), page).T.astype(BF16),
                         preferred_element_type=F32) for pg in range(n_pages)]
        tiles.append(jnp.where(new_visible, lax.dot_general(qs, kn, _NT, preferred_element_type=F32),
                               -jnp.inf))
        m = jnp.max(functools.reduce(jnp.maximum, tiles), axis=-1, keepdims=True)
        ps = [jnp.exp2(st - m) for st in tiles]
        l = jnp.sum(functools.reduce(jnp.add, ps), axis=-1, keepdims=True)
        acc = jnp.dot(ps[n_pages].astype(BF16), vn, preferred_element_type=F32)
        for pg in range(n_pages):
            acc = acc + jnp.dot(ps[pg].astype(BF16), rows_of(vp[pg], (0, 0), page).astype(BF16),
                                preferred_element_type=F32)
        o0, o1 = _diff_finish(acc, l, lam, t, sg_ref[...], 1.0 - lam_init)
        o_scr[h] = jnp.concatenate([o0, o1], axis=1)
        return carry

    lax.fori_loop(0, kvh, head, 0, unroll=8)
    for h in range(kvh):
        o_ref[0, :, 2 * h * LANES:(2 * h + 2) * LANES] = o_scr[h]


def attn_sample(page_table, q, k_new, v_new, cache_k, cache_v, layer, lams, subln, lam_init):
    n, t, _ = q.shape
    kvh = k_new.shape[2]
    n_pages = page_table.shape[1]
    depth, n_pool, page = cache_k.shape[:3]
    k_new, v_new = (a.reshape(n, t * kvh, LANES) for a in (k_new, v_new))
    cache_k, cache_v = (a.reshape(depth, n_pool, page * kvh, LANES) for a in (cache_k, cache_v))

    def page_spec(pg):
        return pl.BlockSpec((1, 1, page * kvh, LANES), lambda b, pt: (layer, pt[b, pg], 0, 0))

    new_spec = pl.BlockSpec((1, t * kvh, LANES), lambda b, pt: (b, 0, 0))
    q_spec = pl.BlockSpec((1, t, q.shape[2]), lambda b, pt: (b, 0, 0))
    vec = pl.BlockSpec((1, DA_DIM), lambda b, pt: (0, 0))
    pages = [page_spec(pg) for pg in range(n_pages)]
    grid_spec = pltpu.PrefetchScalarGridSpec(
        num_scalar_prefetch=1,
        grid=(n,),
        in_specs=[q_spec, new_spec, new_spec] + pages + pages
                 + [vec, vec, vec, vec, pl.BlockSpec((1, LANES), lambda b, pt: (0, 0))],
        out_specs=q_spec,
        scratch_shapes=[pltpu.VMEM((kvh * 4 * t, LANES), BF16), pltpu.VMEM((kvh, t, 2 * LANES), F32)],
    )
    return pl.pallas_call(
        functools.partial(_attn_sample_kernel, n_pages=n_pages, t=t, kvh=kvh, lam_init=lam_init),
        out_shape=jax.ShapeDtypeStruct(q.shape, F32),
        grid_spec=grid_spec,
        compiler_params=_cparams("parallel"),
        name="attn_sample",
    )(page_table, q, k_new, v_new, *([cache_k] * n_pages), *([cache_v] * n_pages), *lams, subln)


def _split3(x):
    hi = x.astype(BF16)
    r1 = x - hi.astype(F32)
    mid = r1.astype(BF16)
    lo = (r1 - mid.astype(F32)).astype(BF16)
    return hi, mid, lo


def _hgrn_chunk(qb, fb, vb, lb, states, group_ones, off_mask, f_scr, k_scr, v_scr, *, sub, valid):
    c_len, w = qb.shape
    hb = w // LANES
    nsub = c_len // sub
    gw = group_ones.shape[0]
    heads = [slice(h * LANES, (h + 1) * LANES) for h in range(hb)]
    f = lb + (1.0 - lb) * jax.nn.sigmoid(fb)
    if valid < c_len:
        f = jnp.where(lax.broadcasted_iota(jnp.int32, (c_len, w), 0) < valid, f, 1.0)
    lf = jnp.log(f)
    kk = 1.0 - f
    qq = _silu(qb)
    vb16 = vb.astype(BF16)
    for h, sl in enumerate(heads):
        f_scr[h] = f[:, sl]
        k_scr[h] = kk[:, sl]
        v_scr[h] = vb[:, sl]

    def row_bcast(ref, r):
        return jnp.concatenate([jnp.broadcast_to(ref[h, pl.ds(r, 1), :], (sub // 2, LANES))
                                for h in range(hb)], axis=1)

    r_i = lax.broadcasted_iota(jnp.int32, (c_len, c_len), 0)
    c_i = lax.broadcasted_iota(jnp.int32, (c_len, c_len), 1)
    tri = (c_i <= r_i).astype(BF16)
    b = functools.reduce(jnp.add, [jnp.dot(tri, part, preferred_element_type=F32) for part in _split3(lf)])
    b_last = b[c_len - 1:c_len, :]
    qe = (qq * jnp.exp(b)).astype(BF16)
    khat = (kk * jnp.exp(b_last - b)).astype(BF16)
    e_last = jnp.exp(b_last)

    if nsub > 1:
        q_parts, k_parts, v_parts = [], [], []
        for i in range(1, nsub):
            lo = i * sub
            b_ref = b[lo - 1:lo, :]
            q_parts.append((qq[lo:lo + sub] * jnp.exp(b[lo:lo + sub] - b_ref)).astype(BF16))
            k_parts.append((kk[:lo] * jnp.exp(b_ref - b[:lo])).astype(BF16))
            v_parts.append(vb16[:lo])
        q_off, k_off, v_off = (jnp.concatenate(p, axis=0) for p in (q_parts, k_parts, v_parts))
        atts = [lax.dot_general(q_off[:, sl], k_off[:, sl], _NT, preferred_element_type=F32) for sl in heads]

    o_inter = [jnp.dot(qe[:, sl], states[h].astype(BF16), preferred_element_type=F32)
               for h, sl in enumerate(heads)]
    upds = [lax.dot_general(khat[:, sl], vb16[:, sl], _TN, preferred_element_type=F32) for sl in heads]

    half = sub // 2
    trow = lax.broadcasted_iota(jnp.int32, (half, w), 0)
    o_diag = []
    for i in range(nsub):
        lo = i * sub
        q_lo, q_hi = qq[lo:lo + half], qq[lo + half:lo + sub]
        g_hi = jnp.where(trow == half - 1, q_hi, 0.0)
        upper = [None] * half
        upper[half - 1] = g_hi * row_bcast(k_scr, lo + sub - 1)
        for s in range(sub - 2, half - 1, -1):
            g_hi = jnp.where(trow == s - half, q_hi, g_hi * row_bcast(f_scr, lo + s + 1))
            upper[s - half] = g_hi * row_bcast(k_scr, lo + s)
        g_lo = jnp.zeros((half, w), F32)
        both = [None] * half
        for s in range(half - 1, -1, -1):
            f_row = row_bcast(f_scr, lo + s + 1)
            k_row = row_bcast(k_scr, lo + s)
            g_hi = g_hi * f_row
            g_lo = jnp.where(trow == s, q_lo, g_lo * f_row)
            both[s] = jnp.concatenate([g_lo * k_row, g_hi * k_row], axis=0)
        p_all = jnp.concatenate(both + upper, axis=0).astype(BF16)
        rsum = jnp.concatenate(
            [jnp.dot(p_all[:, j * gw:(j + 1) * gw], group_ones, preferred_element_type=F32)
             for j in range(w // gw)], axis=1)
        o_lo = jnp.zeros((half, w), F32)
        o_hi = jnp.zeros((half, w), F32)
        for s in range(half):
            v_row = row_bcast(v_scr, lo + s)
            o_lo = o_lo + rsum[s * sub:s * sub + half] * v_row
            o_hi = o_hi + rsum[s * sub + half:(s + 1) * sub] * v_row
        for j in range(half):
            base = half * sub + j * half
            o_hi = o_hi + rsum[base:base + half] * row_bcast(v_scr, lo + half + j)
        o_diag += [o_lo, o_hi]
    o_diag = jnp.concatenate(o_diag, axis=0)

    outs, new_states = [], []
    for h, sl in enumerate(heads):
        o_h = o_inter[h] + o_diag[:, sl]
        if nsub > 1:
            att = jnp.where(off_mask, atts[h], 0.0).astype(BF16)
            o_off = jnp.dot(att, v_off[:, sl], preferred_element_type=F32)
            o_h = o_h + jnp.concatenate([jnp.zeros((sub, LANES), F32), o_off], axis=0)
        outs.append(o_h)
        decay = jnp.broadcast_to(e_last[:, sl], (LANES, LANES)).T
        new_states.append(decay * states[h] + upds[h])
    return outs, new_states


def _hgrn_kernel(q_ref, f_ref, v_ref, go_ref, gb_ref, lb_ref, g_ref, *rest, hb, chunk, sub, has_init):
    if has_init:
        s0_ref, o_ref, s_ref, f_scr, k_scr, v_scr = rest
    else:
        o_ref, s_ref, f_scr, k_scr, v_scr = rest
    tb = pl.program_id(2)

    @pl.when(tb == 0)
    def _():
        if has_init:
            s_ref[...] = s0_ref[...]
        else:
            s_ref[...] = jnp.zeros(s_ref.shape, F32)

    gw = MXU_COLS if (hb * LANES) % MXU_COLS == 0 else LANES
    gr = lax.broadcasted_iota(jnp.int32, (gw, gw), 0) // LANES
    gc = lax.broadcasted_iota(jnp.int32, (gw, gw), 1) // LANES
    group_ones = (gr == gc).astype(BF16)
    nsub = chunk // sub
    off_mask = None
    if nsub > 1:
        shape = ((nsub - 1) * sub, sub * nsub * (nsub - 1) // 2)
        qblk = lax.broadcasted_iota(jnp.int32, shape, 0) // sub + 1
        kcol = lax.broadcasted_iota(jnp.int32, shape, 1)
        off_mask = functools.reduce(jnp.logical_or, [
            (qblk == i) & (kcol >= sub * i * (i - 1) // 2) & (kcol < sub * i * (i + 1) // 2)
            for i in range(1, nsub)])
    t_blk = q_ref.shape[1]

    def run(rows, valid):
        tile = lambda ref: _pad_rows(ref[0, rows, :], chunk)
        outs, new = _hgrn_chunk(tile(q_ref), tile(f_ref), tile(v_ref), lb_ref[...],
                                [s_ref[0, hh] for hh in range(hb)], group_ones, off_mask,
                                f_scr, k_scr, v_scr, sub=sub, valid=valid)
        for hh in range(hb):
            sl = slice(hh * LANES, (hh + 1) * LANES)
            s_ref[0, hh] = new[hh]
            y = _rms(outs[hh][:valid], g_ref[...]) * _silu(go_ref[0, rows, sl])
            o_ref[0, rows, sl] = jax.nn.sigmoid(gb_ref[0, rows, sl]) * y

    if t_blk < chunk:
        run(slice(None), t_blk)
        return

    def body(ci, carry):
        run(pl.ds(pl.multiple_of(ci * chunk, chunk), chunk), chunk)
        return carry

    lax.fori_loop(0, t_blk // chunk, body, 0, unroll=8)


def hgrn(rest, lb, hg_g, s0, layer, *, n, L, tb, hb, chunk, sub):
    d = HG_HEADS * HG_DK
    x3 = rest.reshape(n, L, rest.shape[1])
    nhb = HG_HEADS // hb
    w = hb * LANES
    has_init = s0 is not None

    def col_spec(seg):
        return pl.BlockSpec((1, tb, w), lambda b, h, t: (b, t, seg * (d // w) + h))

    in_specs = [col_spec(0), col_spec(1), col_spec(2), col_spec(3), col_spec(5),
                pl.BlockSpec((1, w), lambda b, h, t: (0, h)),
                pl.BlockSpec((1, LANES), lambda b, h, t: (0, 0))]
    args = [x3, x3, x3, x3, x3, lb, hg_g]
    st_spec = pl.BlockSpec((1, hb, HG_DK, LANES), lambda b, h, t: (b, h, 0, 0))
    if has_init:
        in_specs.append(pl.BlockSpec((1, 1, hb, HG_DK, LANES), lambda b, h, t: (layer, b, h, 0, 0)))
        args.append(s0)
    kernel_fn = functools.partial(_hgrn_kernel, hb=hb, chunk=chunk, sub=sub, has_init=has_init)
    if has_init:
        inner = kernel_fn

        def kernel_fn(q_ref, f_ref, v_ref, go_ref, gb_ref, lb_ref, g_ref, s0_ref, *rest):
            inner(q_ref, f_ref, v_ref, go_ref, gb_ref, lb_ref, g_ref, s0_ref.at[0], *rest)

    return pl.pallas_call(
        kernel_fn,
        out_shape=[jax.ShapeDtypeStruct((n, L, d), F32),
                   jax.ShapeDtypeStruct((n, HG_HEADS, HG_DK, LANES), F32)],
        grid=(n, nhb, L // tb),
        in_specs=in_specs,
        out_specs=[pl.BlockSpec((1, tb, w), lambda b, h, t: (b, t, h)), st_spec],
        scratch_shapes=[pltpu.VMEM((hb, chunk, LANES), F32)] * 3,
        compiler_params=_cparams("parallel", "parallel", "arbitrary"),
        name="hgrn",
    )(*args)


def _out_tail(a, w_ref, x_ref, post_ref, nxt_ref, y_ref, hn_ref):
    t = jnp.dot(a, w_ref[...], preferred_element_type=F32)
    y = x_ref[...] + _rms(t, post_ref[...])
    y_ref[...] = y
    hn_ref[...] = _rms(y, nxt_ref[...]).astype(BF16)


def _outproj_kernel(a_ref, w_ref, x_ref, post_ref, nxt_ref, y_ref, hn_ref):
    _out_tail(a_ref[...].astype(BF16), w_ref, x_ref, post_ref, nxt_ref, y_ref, hn_ref)


def _merge_kernel(ya_ref, yb_ref, ga_ref, w_ref, x_ref, post_ref, nxt_ref, y_ref, hn_ref):
    m = jax.nn.sigmoid(ga_ref[...]) * ya_ref[...].astype(F32) + yb_ref[...]
    _out_tail(m.astype(BF16), w_ref, x_ref, post_ref, nxt_ref, y_ref, hn_ref)


def _out_common(kernel, lead_args, lead_specs, w, x, post, nxt, tm, name):
    rows, d = x.shape
    kdim = w.shape[0]
    vec = pl.BlockSpec((1, d), lambda i: (0, 0))
    row = pl.BlockSpec((tm, d), lambda i: (i, 0))
    return pl.pallas_call(
        kernel,
        out_shape=[jax.ShapeDtypeStruct((rows, d), F32), jax.ShapeDtypeStruct((rows, d), BF16)],
        grid=(rows // tm,),
        in_specs=lead_specs + [pl.BlockSpec((kdim, d), lambda i: (0, 0)), row, vec, vec],
        out_specs=[row, row],
        compiler_params=_cparams("parallel"),
        name=name,
    )(*lead_args, w, x, post, nxt)


def outproj(a, w, x, post, nxt, *, tm):
    spec = pl.BlockSpec((tm, a.shape[1]), lambda i: (i, 0))
    return _out_common(_outproj_kernel, [a], [spec], w, x, post, nxt, tm, "outproj")


def merge(ya, yb, rest, w, x, post, nxt, *, tm):
    d = x.shape[1]
    row = pl.BlockSpec((tm, d), lambda i: (i, 0))
    gate = lambda seg: pl.BlockSpec((tm, d), lambda i: (i, seg))
    return _out_common(_merge_kernel, [ya, yb, rest], [row, row, gate(4)], w, x, post, nxt, tm, "merge")


def _cross_kernel(q_ref, k_ref, v_ref, o_ref, *, nb, heads, native):
    scale = CA_DIM ** -0.5 * LOG2E
    for b in range(nb):
        for h in range(heads):
            sl = slice(h * CA_DIM, (h + 1) * CA_DIM)
            q = q_ref[b, :, sl]
            tq = q.shape[0]
            if tq % BF16_ROWS:
                q = _pad_rows(q.astype(F32), BF16_ROWS * pl.cdiv(tq, BF16_ROWS))
            if native:
                k, v = k_ref[0, b, :, h, :], v_ref[0, b, :, h, :]
            else:
                k, v = k_ref[b, :, sl], v_ref[b, :, sl]
            s = lax.dot_general(q.astype(BF16), k.astype(BF16), _NT, preferred_element_type=F32) * scale
            p = jnp.exp2(s - jnp.max(s, axis=-1, keepdims=True))
            l = jnp.sum(p, axis=-1, keepdims=True)
            o = jnp.dot(p.astype(BF16), v.astype(BF16), preferred_element_type=F32) / l
            o_ref[b, :, sl] = o[:tq].astype(o_ref.dtype)


def cross_attend(q, mk, mv, layer, *, nb, tq):
    n, L, w = q.shape
    native = mk.ndim == 5
    qspec = pl.BlockSpec((nb, tq, w), lambda b, i: (b, i, 0))
    if native:
        kspec = pl.BlockSpec((1, nb) + mk.shape[2:], lambda b, i: (layer, b, 0, 0, 0))
    else:
        kspec = pl.BlockSpec((nb,) + mk.shape[1:], lambda b, i: (b, 0, 0))
    return pl.pallas_call(
        functools.partial(_cross_kernel, nb=nb, heads=w // CA_DIM, native=native),
        out_shape=jax.ShapeDtypeStruct(q.shape, q.dtype),
        grid=(n // nb, L // tq),
        in_specs=[qspec, kspec, kspec],
        out_specs=qspec,
        compiler_params=_cparams("parallel", "arbitrary"),
        name="cross_attend",
    )(q, mk, mv)


def _row(v):
    return v.reshape(1, -1).astype(F32)


def _group_step(x, pos, n, L, layer, lw, lower, attn_fn, s0, mem_k, mem_v, cfg):
    d = x.shape[1]
    tm = cfg["tm"]
    x1, h1 = ffn_half(x, lw["ffn1_pre"], lw["ffn1_post"], lw["mix_pre"], lw["ffn1_w_gu"], lw["ffn1_w_down"],
                      tm=cfg["tm_ffn"], tf=cfg["tf"], emit_next=True)

    tabs = rope_tables(pos)
    period = max(L // tm, 1)
    if L < tm:
        tabs = tuple(jnp.tile(t, (tm // L, 1)) for t in tabs)
    nq = DA_HEADS * 2 * DA_DIM
    nk = DA_KV_HEADS * 2 * DA_DIM
    w_in = lw["w_in"]
    short = L % BF16_ROWS != 0
    (qa,) = project(h1, w_in, 0, nq, tm=tm, tn=cfg["tn"], rope_tabs=tabs, rope_period=period,
                    scale=DA_DIM ** -0.5 * LOG2E, emit_f32=short, emit_bf16=not short)
    kv_out = project(h1, w_in, nq, nk, tm=tm, tn=cfg["tn"], rope_tabs=tabs, rope_period=period,
                     emit_f32=True, emit_bf16=not short)
    vv_out = project(h1, w_in, nq + nk, nk, tm=tm, tn=cfg["tn"], emit_f32=True, emit_bf16=not short)
    k_rows = kv_out[0].reshape(n, L, DA_KV_HEADS, 2 * DA_DIM)
    v_rows = vv_out[0].reshape(n, L, DA_KV_HEADS, 2 * DA_DIM)
    (rest,) = project(h1, w_in, nq + 2 * nk, 6 * d, tm=tm, tn=cfg["tn"], emit_f32=True, emit_bf16=False)

    if short:
        ya = attn_fn(qa.reshape(n, L, nq), k_rows, v_rows)
    else:
        ya = attn_fn(qa.reshape(n, L, nq), kv_out[1].reshape(n, L, nk), vv_out[1].reshape(n, L, nk))
    yb, s_new = hgrn(rest, lower, lw["hg_norm_g"], s0, layer, n=n, L=L, tb=cfg["hg_tb"], hb=cfg["hg_hb"],
                     chunk=cfg["hg_chunk"], sub=cfg["hg_sub"])
    x2, h2 = merge(ya.reshape(n * L, d), yb.reshape(n * L, d), rest, lw["w_out"], x1,
                   lw["mix_post"], lw["ca_pre"], tm=cfg["tm_merge"])

    (qc,) = project(h2, lw["w_cq"], 0, CA_HEADS * CA_DIM, tm=tm, tn=CA_HEADS * CA_DIM,
                    emit_f32=short, emit_bf16=not short)
    oc = cross_attend(qc.reshape(n, L, -1), mem_k, mem_v, layer, nb=cfg["ca_nb"], tq=cfg["ca_tq"])
    x3, _ = outproj(oc.reshape(n * L, -1), lw["w_co"], x2, lw["ca_post"], lw["ffn2_pre"], tm=cfg["tm_merge"])

    x4, _ = ffn_half(x3, lw["ffn2_pre"], lw["ffn2_post"], lw["ffn2_pre"], lw["ffn2_w_gu"], lw["ffn2_w_down"],
                     tm=cfg["tm_ffn"], tf=cfg["tf"], emit_next=False)
    return x4, k_rows, v_rows, s_new


def kernel(x_prompt, x_sample, cache_k, cache_v, state_hgrn, cache_mem_k, cache_mem_v, page_table, mem_prompt, ffn1_pre, ffn1_post, ffn1_w_gu, ffn1_w_down, mix_pre, mix_post, w_in, w_out, lambda_q1, lambda_k1, lambda_q2, lambda_k2, subln_g, hg_norm_g, hg_lb_logits, ca_pre, ca_post, mem_norm_g, w_cq, w_ckv, w_co, ffn2_pre, ffn2_post, ffn2_w_gu, ffn2_w_down):
    n_p, seq, d = x_prompt.shape
    n_s, dec_seq, _ = x_sample.shape
    depth = ffn1_pre.shape[0]
    page = cache_k.shape[2]
    past = page_table.shape[1] * page
    mem_len = mem_prompt.shape[1]
    ca_w = CA_HEADS * CA_DIM

    lower = jnp.cumsum(jax.nn.softmax(hg_lb_logits.astype(F32), axis=0), axis=0)
    xp = x_prompt.reshape(n_p * seq, d)
    xs = x_sample.reshape(n_s * dec_seq, d)
    pos_p = jnp.arange(seq)
    pos_s = past + jnp.arange(dec_seq)

    rows_p, rows_s = n_p * seq, n_s * dec_seq
    cfg_p = dict(tm=min(1024, rows_p), tn=1024, tm_ffn=min(512, rows_p), tf=512, tm_merge=min(256, rows_p),
                 hg_tb=min(512, seq), hg_hb=8, hg_chunk=64, hg_sub=16, ca_nb=1, ca_tq=min(512, seq))
    cfg_s = dict(tm=min(1024, rows_s), tn=1024, tm_ffn=min(512, rows_s), tf=512, tm_merge=min(256, rows_s),
                 hg_tb=dec_seq, hg_hb=16, hg_chunk=BF16_ROWS, hg_sub=BF16_ROWS, ca_nb=min(4, n_s), ca_tq=dec_seq)

    outs = [[] for _ in range(8)]
    for l in range(depth):
        lam_init = 0.8 - 0.6 * math.exp(-0.3 * l)
        lw = {
            "ffn1_pre": _row(ffn1_pre[l]), "ffn1_post": _row(ffn1_post[l]),
            "ffn1_w_gu": ffn1_w_gu[l].astype(BF16), "ffn1_w_down": ffn1_w_down[l].astype(BF16),
            "mix_pre": _row(mix_pre[l]), "mix_post": _row(mix_post[l]),
            "w_in": w_in[l].astype(BF16), "w_out": w_out[l].astype(BF16),
            "hg_norm_g": _row(hg_norm_g[l]),
            "ca_pre": _row(ca_pre[l]), "ca_post": _row(ca_post[l]),
            "w_cq": w_cq[l].astype(BF16), "w_co": w_co[l].astype(BF16),
            "ffn2_pre": _row(ffn2_pre[l]), "ffn2_post": _row(ffn2_post[l]),
            "ffn2_w_gu": ffn2_w_gu[l].astype(BF16), "ffn2_w_down": ffn2_w_down[l].astype(BF16),
        }
        lams = [_row(lambda_q1[l]), _row(lambda_k1[l]), _row(lambda_q2[l]), _row(lambda_k2[l])]
        subln = _row(subln_g[l])
        lb = _row(lower[l])

        rows_m = n_p * mem_len
        hm = rmsnorm_bf16(mem_prompt.reshape(rows_m, d), _row(mem_norm_g[l]), tm=min(512, rows_m))
        w_ckv_l = w_ckv[l].astype(BF16)
        tm_m = min(1024, rows_m)
        mk32, mk16 = project(hm, w_ckv_l, 0, ca_w, tm=tm_m, tn=ca_w, emit_f32=True, emit_bf16=True)
        mv32, mv16 = project(hm, w_ckv_l, ca_w, ca_w, tm=tm_m, tn=ca_w, emit_f32=True, emit_bf16=True)

        attn_p = functools.partial(attn_prompt, lams=lams, subln=subln, lam_init=lam_init,
                                   tq=min(256, seq))
        xp, kp, vp, sp = _group_step(xp, pos_p, n_p, seq, l, lw, lb, attn_p, None,
                                     mk16.reshape(n_p, mem_len, ca_w), mv16.reshape(n_p, mem_len, ca_w), cfg_p)

        def attn_s(q, k, v):
            return attn_sample(page_table, q, k, v, cache_k, cache_v, l, lams, subln, lam_init)

        xs, ks, vs, ss = _group_step(xs, pos_s, n_s, dec_seq, l, lw, lb, attn_s, state_hgrn,
                                     cache_mem_k, cache_mem_v, cfg_s)

        for lst, val in zip(outs, (
                kp, vp, sp, mk32.reshape(n_p, mem_len, CA_HEADS, CA_DIM),
                mv32.reshape(n_p, mem_len, CA_HEADS, CA_DIM), ks, vs, ss)):
            lst.append(val)

    return (xp.reshape(n_p, seq, d), xs.reshape(n_s, dec_seq, d), *[jnp.stack(o) for o in outs])
```
